```python
import math
import jax, jax.numpy as jnp
from jax import lax
import numpy as np

D_MODEL = 1024
BATCH = 2
SEQ = 8192
DEPTH = 1

CHUNK = 64
M_HEADS = 4
M_HEAD_DIM = D_MODEL // M_HEADS
M_WIDTH = M_HEADS * M_HEAD_DIM
CONV_WIDTH = 4
A_HEADS = 8
A_HEAD_DIM = D_MODEL // A_HEADS
A_WIDTH = A_HEADS * A_HEAD_DIM
IDX_HEADS = 8
IDX_DIM = 64
TOPK_MAX = 256
Q_BLOCK = 128
ROPE_THETA = 10000.0
N_EXPERTS = 32
TOP_K = 4
D_FF = D_MODEL
SWIGLU_LIMIT = 7.0
SWIGLU_ALPHA = 1.702
LN_EPS = 1e-5
DEEPNORM_ALPHA = (2.0 * DEPTH) ** 0.25
DEEPNORM_BETA = (8.0 * DEPTH) ** -0.25

COLUMN_SIZES = (M_WIDTH, M_WIDTH, M_WIDTH, M_WIDTH, M_HEADS, M_HEADS,
                A_WIDTH, A_WIDTH, A_WIDTH, IDX_HEADS * IDX_DIM, IDX_DIM, IDX_HEADS,
                D_MODEL, D_MODEL)
IN_WIDTH = 4 * M_WIDTH + 2 * M_HEADS + 3 * A_WIDTH + IDX_HEADS * IDX_DIM + IDX_DIM + IDX_HEADS + 2 * D_MODEL
F_GATE_OFFSET = 4 * M_WIDTH + M_HEADS

kernel_name = 'hybrid_mlstm_dsa_moe_block'


def _split_columns(p):
    out, start = [], 0
    for s in COLUMN_SIZES:
        out.append(p[..., start:start + s])
        start += s
    return out


def _layer_norm(x, g, b):
    xf = x.astype(jnp.float32)
    mu = xf.mean(-1, keepdims=True)
    var = jnp.square(xf - mu).mean(-1, keepdims=True)
    return ((xf - mu) * lax.rsqrt(var + LN_EPS) * g + b).astype(x.dtype)


def _head_norm(h, g):
    mu = h.mean(-1, keepdims=True)
    var = jnp.square(h - mu).mean(-1, keepdims=True)
    return (h - mu) * lax.rsqrt(var + LN_EPS) * g


def _rope(x, pos):
    half = x.shape[-1] // 2
    inv = ROPE_THETA ** (-jnp.arange(half, dtype=jnp.float32) / half)
    ang = pos.astype(jnp.float32)[:, None] * inv[None, :]
    cos = jnp.cos(ang)[None, :, None, :]
    sin = jnp.sin(ang)[None, :, None, :]
    x1 = x[..., :half].astype(jnp.float32)
    x2 = x[..., half:].astype(jnp.float32)
    return jnp.concatenate([x1 * cos - x2 * sin, x2 * cos + x1 * sin], -1).astype(x.dtype)


def _causal_conv(x, w, b):
    K, T = w.shape[0], x.shape[1]
    xp = jnp.pad(x, ((0, 0), (K - 1, 0), (0, 0)))
    y = b
    for j in range(K):
        y = y + xp[:, j:j + T] * w[j]
    return y


def _mlstm(q, k, v, i_pre, f_pre):
    B, T, H, d = q.shape
    nc = T // CHUNK

    def to_chunks(a):
        a = a.reshape((B, nc, CHUNK) + a.shape[2:])
        return jnp.moveaxis(a, (1, 3), (0, 2))

    logf = jax.nn.log_sigmoid(f_pre)
    xs = tuple(to_chunks(a) for a in (q, k, v, i_pre, logf))
    causal = jnp.tril(jnp.ones((CHUNK, CHUNK), dtype=bool))

    def step(carry, xs_c):
        C, n, m = carry
        qb, kb, vb, ib, fb = xs_c
        b = jnp.cumsum(fb, axis=-1)
        Dm = b[..., :, None] - b[..., None, :] + ib[..., None, :]
        Dm = jnp.where(causal, Dm, -jnp.inf)
        m_inter = b + m[..., None]
        m_t = jnp.maximum(m_inter, Dm.max(-1))
        S = jnp.einsum('bhtd,bhsd->bhts', qb, kb) * jnp.exp(Dm - m_t[..., None])
        w_inter = jnp.exp(m_inter - m_t)
        num = jnp.einsum('bhts,bhsd->bhtd', S, vb) + w_inter[..., None] * jnp.einsum('bhtk,bhkv->bhtv', qb, C)
        den = S.sum(-1) + w_inter * jnp.einsum('bhtk,bhk->bht', qb, n)
        h = num / jnp.maximum(jnp.abs(den), jnp.exp(-m_t))[..., None]
        bL = b[..., -1]
        g = bL[..., None] - b + ib
        m_new = jnp.maximum(bL + m, g.max(-1))
        wg = jnp.exp(g - m_new[..., None])
        decay = jnp.exp(bL + m - m_new)
        C_new = decay[..., None, None] * C + jnp.einsum('bhs,bhsk,bhsv->bhkv', wg, kb, vb)
        n_new = decay[..., None] * n + jnp.einsum('bhs,bhsk->bhk', wg, kb)
        return (C_new, n_new, m_new), h

    init = (jnp.zeros((B, H, d, d), jnp.float32), jnp.zeros((B, H, d), jnp.float32),
            jnp.zeros((B, H), jnp.float32))
    _, hs = lax.scan(step, init, xs)
    return hs.transpose(1, 0, 3, 2, 4).reshape(B, T, H, d)


def _dsa(q, k, v, q_idx, k_idx, w_idx):
    B, T, H, dh = q.shape
    top_k = min(TOPK_MAX, T // 4)
    key_chunk = jnp.arange(T) // CHUNK
    bidx = jnp.arange(B)[:, None, None]
    k_idx_f = k_idx.astype(jnp.float32)

    def block(start):
        qb = lax.dynamic_slice_in_dim(q, start, Q_BLOCK, axis=1)
        qib = lax.dynamic_slice_in_dim(q_idx, start, Q_BLOCK, axis=1).astype(jnp.float32)
        wib = lax.dynamic_slice_in_dim(w_idx, start, Q_BLOCK, axis=1).astype(jnp.float32)
        q_chunk = (start + jnp.arange(Q_BLOCK)) // CHUNK
        s_idx = jnp.einsum('bqhd,bsd->bqhs', qib, k_idx_f) * IDX_DIM ** -0.5
        scores = jnp.einsum('bqh,bqhs->bqs', wib * IDX_HEADS ** -0.5, jax.nn.relu(s_idx))
        admissible = key_chunk[None, :] <= q_chunk[:, None]
        scores = jnp.where(admissible[None], scores, -jnp.inf)
        _, sel = lax.top_k(scores, top_k)
        valid = (sel // CHUNK) <= q_chunk[None, :, None]
        k_sel = k[bidx, sel]
        v_sel = v[bidx, sel]
        logits = jnp.einsum('bqhd,bqkhd->bqhk', qb, k_sel).astype(jnp.float32) * dh ** -0.5
        logits = jnp.where(valid[:, :, None, :], logits, -jnp.inf)
        p = jax.nn.softmax(logits, axis=-1)
        return jnp.einsum('bqhk,bqkhd->bqhd', p.astype(v.dtype), v_sel)

    starts = jnp.arange(T // Q_BLOCK) * Q_BLOCK
    out = lax.map(block, starts)
    return out.transpose(1, 0, 2, 3, 4).reshape(B, T, H, dh)


def _moe(h, w_router, b_router, w_gate, b_gate, w_up, b_up, w_down, b_down):
    B, T, D = h.shape
    xt = h.reshape(B * T, D)
    logits = (xt @ w_router + b_router).astype(jnp.float32)
    top_vals, top_idx = lax.top_k(logits, TOP_K)
    probs = jax.nn.softmax(top_vals, axis=-1)
    gates = jnp.sum(jax.nn.one_hot(top_idx, N_EXPERTS, dtype=jnp.float32) * probs[..., None], axis=1)
    y = jnp.zeros((B * T, D), jnp.float32)
    for e in range(N_EXPERTS):
        g = jnp.minimum(xt @ w_gate[e] + b_gate[e], SWIGLU_LIMIT)
        u = jnp.clip(xt @ w_up[e] + b_up[e], -SWIGLU_LIMIT, SWIGLU_LIMIT)
        act = (u + 1.0) * g * jax.nn.sigmoid(SWIGLU_ALPHA * g)
        y = y + gates[:, e:e + 1] * (act @ w_down[e] + b_down[e])
    return y.astype(h.dtype).reshape(B, T, D)


def setup_inputs(seed: int = 0) -> dict:
    key = jax.random.key(seed)
    ks = jax.random.split(key, 24)
    f32 = jnp.float32
    nrm = lambda k, s: jax.random.normal(k, s, f32)
    b_in = 0.02 * nrm(ks[4], (DEPTH, IN_WIDTH))
    b_in = b_in.at[:, F_GATE_OFFSET:F_GATE_OFFSET + M_HEADS].add(jnp.linspace(3.0, 6.0, M_HEADS))
    return {
        'x': nrm(ks[0], (BATCH, SEQ, D_MODEL)),
        'ln_in_g': 1.0 + 0.02 * nrm(ks[1], (D_MODEL,)),
        'ln_in_b': 0.02 * nrm(ks[2], (D_MODEL,)),
        'w_in': nrm(ks[3], (DEPTH, D_MODEL, IN_WIDTH)) * D_MODEL ** -0.5,
        'b_in': b_in,
        'conv_w': nrm(ks[5], (DEPTH, CONV_WIDTH, 2 * M_WIDTH)) * CONV_WIDTH ** -0.5,
        'conv_b': 0.02 * nrm(ks[6], (DEPTH, 2 * M_WIDTH)),
        'm_norm_g': 1.0 + 0.02 * nrm(ks[7], (DEPTH, M_WIDTH)),
        'w_out': nrm(ks[8], (DEPTH, D_MODEL, D_MODEL)) * D_MODEL ** -0.5 * DEEPNORM_BETA,
        'ln1_g': 1.0 + 0.02 * nrm(ks[9], (DEPTH, D_MODEL)),
        'ln1_b': 0.02 * nrm(ks[10], (DEPTH, D_MODEL)),
        'w_router': nrm(ks[11], (DEPTH, D_MODEL, N_EXPERTS)) * D_MODEL ** -0.5,
        'b_router': 0.01 * nrm(ks[12], (DEPTH, N_EXPERTS)),
        'w_gate': nrm(ks[13], (DEPTH, N_EXPERTS, D_MODEL, D_FF)) * D_MODEL ** -0.5,
        'b_gate': 0.01 * nrm(ks[14], (DEPTH, N_EXPERTS, D_FF)),
        'w_up': nrm(ks[15], (DEPTH, N_EXPERTS, D_MODEL, D_FF)) * D_MODEL ** -0.5,
        'b_up': 0.01 * nrm(ks[16], (DEPTH, N_EXPERTS, D_FF)),
        'w_down': nrm(ks[17], (DEPTH, N_EXPERTS, D_FF, D_MODEL)) * D_FF ** -0.5 * DEEPNORM_BETA,
        'b_down': 0.01 * nrm(ks[18], (DEPTH, N_EXPERTS, D_MODEL)),
        'ln2_g': 1.0 + 0.02 * nrm(ks[19], (DEPTH, D_MODEL)),
        'ln2_b': 0.02 * nrm(ks[20], (DEPTH, D_MODEL)),
    }


def reference(x, ln_in_g, ln_in_b, w_in, b_in, conv_w, conv_b, m_norm_g, w_out,
              ln1_g, ln1_b, w_router, b_router, w_gate, b_gate, w_up, b_up,
              w_down, b_down, ln2_g, ln2_b):
    B, T, _ = x.shape
    f32 = jnp.float32
    pos = jnp.arange(T, dtype=jnp.int32)
    h = _layer_norm(x, ln_in_g, ln_in_b)
    for l in range(DEPTH):
        proj = h @ w_in[l] + b_in[l]
        (m_q, m_k, m_v, m_o, m_i, m_f, a_q, a_k, a_v, x_q, x_k, x_w, g_m, g_a) = _split_columns(proj)
        qk = jax.nn.silu(_causal_conv(jnp.concatenate([m_q, m_k], -1), conv_w[l], conv_b[l]))
        mq = qk[..., :M_WIDTH].reshape(B, T, M_HEADS, M_HEAD_DIM) * M_HEAD_DIM ** -0.5
        mk = qk[..., M_WIDTH:].reshape(B, T, M_HEADS, M_HEAD_DIM)
        mv = m_v.reshape(B, T, M_HEADS, M_HEAD_DIM)
        hm = _mlstm(mq.astype(f32), mk.astype(f32), mv.astype(f32), m_i.astype(f32), m_f.astype(f32))
        hm = _head_norm(hm, m_norm_g[l].reshape(M_HEADS, M_HEAD_DIM))
        y_m = jax.nn.sigmoid(m_o) * hm.reshape(B, T, M_WIDTH).astype(h.dtype)
        aq = _rope(a_q.reshape(B, T, A_HEADS, A_HEAD_DIM), pos)
        ak = _rope(a_k.reshape(B, T, A_HEADS, A_HEAD_DIM), pos)
        av = a_v.reshape(B, T, A_HEADS, A_HEAD_DIM)
        iq = _rope(x_q.reshape(B, T, IDX_HEADS, IDX_DIM), pos)
        ik = _rope(x_k[:, :, None, :], pos)[:, :, 0, :]
        y_a = _dsa(aq, ak, av, iq, ik, x_w).reshape(B, T, A_WIDTH)
        merged = jax.nn.sigmoid(g_m) * y_m + jax.nn.sigmoid(g_a) * y_a
        h = _layer_norm(DEEPNORM_ALPHA * h + merged @ w_out[l], ln1_g[l], ln1_b[l])
        moe_out = _moe(h, w_router[l], b_router[l], w_gate[l], b_gate[l], w_up[l], b_up[l], w_down[l], b_down[l])
        h = _layer_norm(DEEPNORM_ALPHA * h + moe_out, ln2_g[l], ln2_b[l])
    return h
```

```python
import functools
import math

import jax
import jax.numpy as jnp
from jax import lax
from jax.experimental import pallas as pl
from jax.experimental.pallas import tpu as pltpu

F32 = jnp.float32
BF16 = jnp.bfloat16
I32 = jnp.int32

D_MODEL = 1024
CHUNK = 64
M_HEADS = 4
M_HEAD_DIM = D_MODEL // M_HEADS
M_WIDTH = M_HEADS * M_HEAD_DIM
CONV_WIDTH = 4
A_HEADS = 8
A_HEAD_DIM = D_MODEL // A_HEADS
A_WIDTH = A_HEADS * A_HEAD_DIM
IDX_HEADS = 8
IDX_DIM = 64
IDX_WIDTH = IDX_HEADS * IDX_DIM
TOPK_MAX = 256
ROPE_THETA = 10000.0
N_EXPERTS = 32
TOP_K = 4
D_FF = D_MODEL
SWIGLU_LIMIT = 7.0
SWIGLU_ALPHA = 1.702
LN_EPS = 1e-5

LANES = 128
NEG_BIG = -1e30
VMEM_LIMIT = 56 * 1024 * 1024

_BIG_GROUPS = ("m_q", "m_k", "m_v", "m_o", "a_q", "a_k", "a_v", "x_q", "g_m", "g_a")
_GROUP_WIDTH = dict(m_q=M_WIDTH, m_k=M_WIDTH, m_v=M_WIDTH, m_o=M_WIDTH, a_q=A_WIDTH, a_k=A_WIDTH,
                    a_v=A_WIDTH, x_q=IDX_WIDTH, g_m=D_MODEL, g_a=D_MODEL, x_k=LANES, small=LANES)
_GROUP_ORDER = _BIG_GROUPS + ("x_k", "small")
_GROUP_START = {}
_off = 0
for _g in _GROUP_ORDER:
    _GROUP_START[_g] = _off
    _off += _GROUP_WIDTH[_g]
PACKED_WIDTH = _off


def _cparams(sem, vmem=VMEM_LIMIT):
    return pltpu.CompilerParams(dimension_semantics=sem, vmem_limit_bytes=vmem)


def _layer_norm(x, g, b):
    mu = jnp.mean(x, axis=-1, keepdims=True)
    xc = x - mu
    var = jnp.mean(xc * xc, axis=-1, keepdims=True)
    return xc * lax.rsqrt(var + LN_EPS) * g + b


def _sigmoid(x):
    return 1.0 / (1.0 + jnp.exp(-x))


def _in_proj_kernel(x_ref, g_ref, b_ref, w_ref, bias_ref, cosa_ref, sina_ref, cosb_ref, sinlo_ref, sinhi_ref,
                    h0_ref, mq_ref, mk_ref, mv_ref, mo_ref, aq_ref, ak_ref, av_ref, xq_ref, gm_ref, ga_ref,
                    xk_ref, small_ref):
    h0 = _layer_norm(x_ref[...], g_ref[...], b_ref[...])
    h0_ref[...] = h0
    hb = h0.astype(BF16)

    def proj(name, c0=0, width=None):
        start = _GROUP_START[name] + c0
        width = _GROUP_WIDTH[name] if width is None else width
        y = jnp.dot(hb, w_ref[:, start:start + width], preferred_element_type=F32)
        return y + bias_ref[:, start:start + width]

    def rope_full(y):
        return y * cosa_ref[...] + pltpu.roll(y, 64, axis=1) * sina_ref[...]

    def rope_half(y):
        return (y * cosb_ref[...] + pltpu.roll(y, 96, axis=1) * sinlo_ref[...]
                + pltpu.roll(y, 32, axis=1) * sinhi_ref[...])

    mq_ref[...] = proj("m_q").astype(BF16)
    mk_ref[...] = proj("m_k").astype(BF16)
    mv_ref[...] = proj("m_v").astype(BF16)
    mo_ref[...] = _sigmoid(proj("m_o")).astype(BF16)
    gm_ref[...] = _sigmoid(proj("g_m")).astype(BF16)
    ga_ref[...] = _sigmoid(proj("g_a")).astype(BF16)
    av_ref[...] = proj("a_v").astype(BF16)
    q_scale = A_HEAD_DIM ** -0.5
    for h in range(A_HEADS):
        sl = slice(h * LANES, (h + 1) * LANES)
        aq_ref[:, sl] = (rope_full(proj("a_q", h * LANES, LANES)) * q_scale).astype(BF16)
        ak_ref[:, sl] = rope_full(proj("a_k", h * LANES, LANES)).astype(BF16)
    for c in range(IDX_WIDTH // LANES):
        sl = slice(c * LANES, (c + 1) * LANES)
        xq_ref[:, sl] = rope_half(proj("x_q", c * LANES, LANES)).astype(BF16)
    xk_ref[...] = rope_half(proj("x_k")).astype(BF16)
    small_ref[...] = proj("small")


def _rope_tables(T):
    pos = jnp.arange(T, dtype=F32)[:, None]
    half_a = A_HEAD_DIM // 2
    inv_a = ROPE_THETA ** (-jnp.arange(half_a, dtype=F32) / half_a)
    ang_a = pos * inv_a[None, :]
    cos_a, sin_a = jnp.cos(ang_a), jnp.sin(ang_a)
    cosa = jnp.concatenate([cos_a, cos_a], axis=1)
    sina = jnp.concatenate([-sin_a, sin_a], axis=1)
    half_b = IDX_DIM // 2
    inv_b = ROPE_THETA ** (-jnp.arange(half_b, dtype=F32) / half_b)
    ang_b = pos * inv_b[None, :]
    cos_b, sin_b = jnp.cos(ang_b), jnp.sin(ang_b)
    zero = jnp.zeros_like(sin_b)
    cosb = jnp.concatenate([cos_b, cos_b, cos_b, cos_b], axis=1)
    sinlo = jnp.concatenate([-sin_b, zero, -sin_b, zero], axis=1)
    sinhi = jnp.concatenate([zero, sin_b, zero, sin_b], axis=1)
    return cosa, sina, cosb, sinlo, sinhi


def _pack_in_weights(w, b):
    sizes = (M_WIDTH, M_WIDTH, M_WIDTH, M_WIDTH, M_HEADS, M_HEADS, A_WIDTH, A_WIDTH, A_WIDTH,
             IDX_WIDTH, IDX_DIM, IDX_HEADS, D_MODEL, D_MODEL)
    names = ("m_q", "m_k", "m_v", "m_o", "m_i", "m_f", "a_q", "a_k", "a_v", "x_q", "x_k", "x_w", "g_m", "g_a")
    parts, start = {}, 0
    for n, s in zip(names, sizes):
        parts[n] = (w[:, start:start + s], b[start:start + s])
        start += s
    rows = w.shape[0]

    def pad(n_cols):
        return jnp.zeros((rows, n_cols), w.dtype), jnp.zeros((n_cols,), b.dtype)

    order = [parts[n] for n in _BIG_GROUPS]
    order += [parts["x_k"], pad(LANES - IDX_DIM)]
    order += [parts["m_i"], parts["m_f"], parts["x_w"], pad(LANES - 2 * M_HEADS - IDX_HEADS)]
    wp = jnp.concatenate([o[0] for o in order], axis=1).astype(BF16)
    bp = jnp.concatenate([o[1] for o in order], axis=0).astype(F32)[None, :]
    return wp, bp


def _in_proj(x2, ln_g, ln_b, wp, bp, T, tm):
    N = x2.shape[0]
    nt = T // tm
    tables = _rope_tables(T)
    row = lambda i: (i, 0)
    const = lambda i: (0, 0)
    tab = lambda i: (i % nt, 0)
    widths = [D_MODEL, M_WIDTH, M_WIDTH, M_WIDTH, M_WIDTH, A_WIDTH, A_WIDTH, A_WIDTH, IDX_WIDTH, D_MODEL, D_MODEL,
              LANES, LANES]
    dtypes = [F32] + [BF16] * 11 + [F32]
    return pl.pallas_call(
        _in_proj_kernel,
        grid=(N // tm,),
        in_specs=[pl.BlockSpec((tm, D_MODEL), row),
                  pl.BlockSpec((1, D_MODEL), const), pl.BlockSpec((1, D_MODEL), const),
                  pl.BlockSpec((D_MODEL, PACKED_WIDTH), const, pipeline_mode=pl.Buffered(1)),
                  pl.BlockSpec((1, PACKED_WIDTH), const)]
                 + [pl.BlockSpec((tm, LANES), tab)] * 5,
        out_specs=[pl.BlockSpec((tm, wd), row) for wd in widths],
        out_shape=[jax.ShapeDtypeStruct((N, wd), dt) for wd, dt in zip(widths, dtypes)],
        compiler_params=_cparams(("parallel",)),
        name="in_proj",
    )(x2, ln_g[None, :], ln_b[None, :], wp, bp, *tables)


def _log_sigmoid(x):
    return jnp.minimum(x, 0.0) - jnp.log(1.0 + jnp.exp(-jnp.abs(x)))


def _mlstm_kernel(q_ref, k_ref, v_ref, o_ref, gm_ref, gcol_ref, grow_ref, cwq_ref, cwk_ref, cbq_ref, cbk_ref,
                  ng_ref, out_ref, qext, kext, c_state, n_state, m_state, *, L):
    c = pl.program_id(2)
    halo = 8

    @pl.when(c == 0)
    def _():
        qext[0:halo, :] = jnp.zeros((halo, M_HEAD_DIM), F32)
        kext[0:halo, :] = jnp.zeros((halo, M_HEAD_DIM), F32)
        c_state[...] = jnp.zeros_like(c_state)
        n_state[...] = jnp.zeros_like(n_state)
        m_state[...] = jnp.zeros_like(m_state)

    def conv_silu(ext, x_ref, w_ref, b_ref):
        x = x_ref[...].astype(F32)
        ext[halo:halo + L, :] = x
        acc = b_ref[...] + ext[pl.ds(halo - 3, L), :] * w_ref[0:1, :]
        for j in range(1, CONV_WIDTH):
            acc = acc + ext[pl.ds(halo - 3 + j, L), :] * w_ref[j:j + 1, :]
        ext[0:halo, :] = x[L - halo:L, :]
        return acc * _sigmoid(acc)

    qc = conv_silu(qext, q_ref, cwq_ref, cbq_ref) * (M_HEAD_DIM ** -0.5)
    kc = conv_silu(kext, k_ref, cwk_ref, cbk_ref)
    v = v_ref[...]

    gcol = gcol_ref[0, 0]
    grow = grow_ref[0, 0]
    i_col, lf_col = gcol[:, 0:1], _log_sigmoid(gcol[:, 1:2])
    i_row, lf_row = grow[0:1, :], _log_sigmoid(grow[1:2, :])

    r = lax.broadcasted_iota(I32, (L, L), 0)
    s = lax.broadcasted_iota(I32, (L, L), 1)
    causal = r >= s
    hi = lax.Precision.HIGHEST
    b_col = jnp.dot(causal.astype(F32), jnp.broadcast_to(lf_col, (L, LANES)), precision=hi,
                    preferred_element_type=F32)[:, 0:1]
    b_row = jnp.dot(jnp.broadcast_to(lf_row, (8, L)), (r <= s).astype(F32), precision=hi,
                    preferred_element_type=F32)[0:1, :]

    m_prev = m_state[...]
    dm = jnp.where(causal, b_col - b_row + i_row, -jnp.inf)
    m_inter = b_col + m_prev
    m_t = jnp.maximum(m_inter, jnp.max(dm, axis=1, keepdims=True))
    qb, kb = qc.astype(BF16), kc.astype(BF16)
    qk = lax.dot_general(qb, kb, (((1,), (1,)), ((), ())), preferred_element_type=F32)
    sm = qk * jnp.exp(dm - m_t)
    w_inter = jnp.exp(m_inter - m_t)
    cb = c_state[...].astype(BF16)
    num = (jnp.dot(sm.astype(BF16), v, preferred_element_type=F32)
           + w_inter * jnp.dot(qb, cb, preferred_element_type=F32))
    den = (jnp.sum(sm, axis=1, keepdims=True)
           + w_inter * jnp.sum(qc * n_state[...], axis=1, keepdims=True))
    hh = num / jnp.maximum(jnp.abs(den), jnp.exp(-m_t))

    b_last = b_col[L - 1:L, :]
    g_row = b_last - b_row + i_row
    m_new = jnp.maximum(b_last + m_prev, jnp.max(g_row, axis=1, keepdims=True))
    wg_col = jnp.exp(b_last - b_col + i_col - m_new)
    decay = jnp.exp(b_last + m_prev - m_new)
    kw = kc * wg_col
    kv = lax.dot_general(kw.astype(BF16), v, (((0,), (0,)), ((), ())), preferred_element_type=F32)
    c_state[...] = decay * c_state[...] + kv
    n_state[...] = decay * n_state[...] + jnp.sum(kw, axis=0, keepdims=True)
    m_state[...] = m_new

    mu = jnp.mean(hh, axis=1, keepdims=True)
    hc = hh - mu
    var = jnp.mean(hc * hc, axis=1, keepdims=True)
    hn = hc * lax.rsqrt(var + LN_EPS) * ng_ref[...]
    out_ref[...] = (hn * o_ref[...].astype(F32) * gm_ref[...].astype(F32)).astype(BF16)


def _mlstm(mq, mk, mv, mo_sig, gm_sig, small, conv_w, conv_b, norm_g, B, T, L):
    nL = T // L
    i_pre = small[:, 0:M_HEADS].reshape(B, T, M_HEADS)
    f_pre = small[:, M_HEADS:2 * M_HEADS].reshape(B, T, M_HEADS)
    gates = jnp.stack([i_pre, f_pre], axis=-1)
    gcol = gates.transpose(0, 2, 1, 3)
    grow = gates.transpose(0, 2, 3, 1)
    blk = lambda b, h, c: (b * nL + c, h)
    head = lambda b, h, c: (0, h)
    kern = functools.partial(_mlstm_kernel, L=L)
    return pl.pallas_call(
        kern,
        grid=(B, M_HEADS, nL),
        in_specs=[pl.BlockSpec((L, M_HEAD_DIM), blk)] * 5
                 + [pl.BlockSpec((1, 1, L, 2), lambda b, h, c: (b, h, c, 0)),
                    pl.BlockSpec((1, 1, 2, L), lambda b, h, c: (b, h, 0, c)),
                    pl.BlockSpec((CONV_WIDTH, M_HEAD_DIM), head),
                    pl.BlockSpec((CONV_WIDTH, M_HEAD_DIM), head),
                    pl.BlockSpec((1, M_HEAD_DIM), head),
                    pl.BlockSpec((1, M_HEAD_DIM), head),
                    pl.BlockSpec((1, M_HEAD_DIM), head)],
        out_specs=pl.BlockSpec((L, M_HEAD_DIM), blk),
        out_shape=jax.ShapeDtypeStruct((B * T, M_WIDTH), BF16),
        scratch_shapes=[pltpu.VMEM((L + 8, M_HEAD_DIM), F32), pltpu.VMEM((L + 8, M_HEAD_DIM), F32),
                        pltpu.VMEM((M_HEAD_DIM, M_HEAD_DIM), F32), pltpu.VMEM((1, M_HEAD_DIM), F32),
                        pltpu.VMEM((1, 1), F32)],
        compiler_params=_cparams(("parallel", "parallel", "arbitrary")),
        name="mlstm",
    )(mq, mk, mv, mo_sig, gm_sig, gcol, grow,
      conv_w[:, :M_WIDTH], conv_w[:, M_WIDTH:], conv_b[None, :M_WIDTH], conv_b[None, M_WIDTH:], norm_g[None, :])


def _select_kernel(xq_ref, xk_ref, small_ref, mask_ref,
                   sc_ref, w_ref, lo_ref, hi_ref, clo_ref, chi_ref, t_ref, j_ref, *, TQ, T, top_k):
    qi = pl.program_id(1)
    nkt = qi + 1
    RB = 64
    nrb = TQ // RB
    kf = float(top_k)
    w_scale = (IDX_DIM ** -0.5) * (IDX_HEADS ** -0.5)

    wsm = small_ref[...]
    for h in range(IDX_HEADS):
        col = 2 * M_HEADS + h
        w_ref[h] = jnp.broadcast_to(wsm[:, col:col + 1] * w_scale, (TQ, LANES))

    row_chunk = lax.broadcasted_iota(I32, (TQ, TQ), 0) // CHUNK
    col_chunk = lax.broadcasted_iota(I32, (TQ, TQ), 1) // CHUNK
    diag_ok = col_chunk <= row_chunk

    def score_tile(kt, carry):
        mx, mn = carry
        koff = pl.multiple_of(kt * TQ, TQ)
        kblk = xk_ref[pl.ds(koff, TQ), 0:IDX_DIM]
        acc = jnp.zeros((TQ, TQ), F32)
        for h in range(IDX_HEADS):
            qh = xq_ref[:, h * IDX_DIM:(h + 1) * IDX_DIM]
            sh = lax.dot_general(qh, kblk, (((1,), (1,)), ((), ())), preferred_element_type=F32)
            wh = w_ref[h]
            acc = acc + jnp.maximum(sh, 0.0) * jnp.concatenate([wh] * (TQ // LANES), axis=1)
        ok = jnp.logical_or(kt < qi, diag_ok)
        sc_ref[:, pl.ds(koff, TQ)] = jnp.where(ok, acc, NEG_BIG)
        mx = jnp.maximum(mx, jnp.max(jnp.where(ok, acc, NEG_BIG), axis=1, keepdims=True))
        mn = jnp.minimum(mn, jnp.min(jnp.where(ok, acc, -NEG_BIG), axis=1, keepdims=True))
        return mx, mn

    mx, mn = lax.fori_loop(0, nkt, score_tile,
                           (jnp.full((TQ, 1), NEG_BIG, F32), jnp.full((TQ, 1), -NEG_BIG, F32)))

    def count_pass(pred):
        outs = []
        for rb in range(nrb):
            rows = slice(rb * RB, (rb + 1) * RB)

            def body(kt, cnt):
                koff = pl.multiple_of(kt * TQ, TQ)
                for cc in range(TQ // LANES):
                    sv = sc_ref[rows, pl.ds(koff + cc * LANES, LANES)]
                    idx = (koff + cc * LANES + lax.broadcasted_iota(I32, (RB, LANES), 1)).astype(F32)
                    cnt = cnt + jnp.where(pred(sv, idx, rows), 1.0, 0.0)
                return cnt

            cnt = lax.fori_loop(0, nkt, body, jnp.zeros((RB, LANES), F32))
            outs.append(jnp.broadcast_to(jnp.sum(cnt, axis=1, keepdims=True), (RB, LANES)))
        return jnp.concatenate(outs, axis=0)

    n_adm = ((qi * TQ + lax.broadcasted_iota(I32, (TQ, LANES), 0)) // CHUNK + 1) * CHUNK
    lo_ref[...] = jnp.broadcast_to(mn, (TQ, LANES))
    hi0 = mx + jnp.maximum(jnp.abs(mx), 1e-30) * 1e-3
    hi_ref[...] = jnp.broadcast_to(hi0, (TQ, LANES))
    clo_ref[...] = n_adm.astype(F32)
    chi_ref[...] = jnp.zeros((TQ, LANES), F32)

    def active_rows():
        lo, hi = lo_ref[...], hi_ref[...]
        mid = 0.5 * lo + 0.5 * hi
        open_iv = jnp.logical_and(mid > lo, mid < hi)
        return jnp.logical_and(clo_ref[...] > kf, open_iv), mid

    def any_rows(flag):
        return jnp.max(jnp.where(flag, 1.0, 0.0)) > 0.0

    def cond(carry):
        it, go = carry
        return jnp.logical_and(go, it < 400)

    def body(carry):
        it, _ = carry
        act, mid = active_rows()
        t_ref[...] = mid
        cnt = count_pass(lambda sv, idx, rows: sv >= t_ref[rows, :])
        ge = cnt >= kf
        up = jnp.logical_and(act, ge)
        dn = jnp.logical_and(act, jnp.logical_not(ge))
        lo_ref[...] = jnp.where(up, mid, lo_ref[...])
        clo_ref[...] = jnp.where(up, cnt, clo_ref[...])
        hi_ref[...] = jnp.where(dn, mid, hi_ref[...])
        chi_ref[...] = jnp.where(dn, cnt, chi_ref[...])
        act2, _ = active_rows()
        return it + 1, any_rows(act2)

    act0, _ = active_rows()
    lax.while_loop(cond, body, (jnp.int32(0), any_rows(act0)))

    tie = clo_ref[...] > kf
    has_tie = any_rows(tie)
    j_ref[...] = jnp.full((TQ, LANES), float(T), F32)

    @pl.when(has_tie)
    def _():
        need = kf - chi_ref[...]
        jlo = jnp.zeros((TQ, LANES), F32)
        jhi = jnp.full((TQ, LANES), float(T), F32)
        for _i in range(int(math.log2(T)) + 1):
            jmid = jnp.floor(0.5 * (jlo + jhi))
            t_ref[...] = jmid
            cnt = count_pass(lambda sv, idx, rows: jnp.logical_and(
                jnp.logical_and(sv >= lo_ref[rows, :], sv < hi_ref[rows, :]), idx < t_ref[rows, :]))
            ge = cnt >= need
            jhi = jnp.where(ge, jmid, jhi)
            jlo = jnp.where(ge, jlo, jmid)
        j_ref[...] = jnp.where(tie, jhi, float(T))

    def write_tile(kt, _):
        koff = pl.multiple_of(kt * TQ, TQ)
        for rb in range(nrb):
            rows = slice(rb * RB, (rb + 1) * RB)
            for cc in range(TQ // LANES):
                sv = sc_ref[rows, pl.ds(koff + cc * LANES, LANES)]
                idx = (koff + cc * LANES + lax.broadcasted_iota(I32, (RB, LANES), 1)).astype(F32)
                keep = jnp.logical_and(sv >= lo_ref[rows, :],
                                       jnp.logical_or(sv >= hi_ref[rows, :], idx < j_ref[rows, :]))
                mask_ref[rows, pl.ds(koff + cc * LANES, LANES)] = jnp.where(keep, 1.0, 0.0).astype(jnp.int8)
        return 0

    lax.fori_loop(0, nkt, write_tile, 0)

    def zero_tile(kt, _):
        koff = pl.multiple_of(kt * TQ, TQ)
        mask_ref[:, pl.ds(koff, TQ)] = jnp.zeros((TQ, TQ), jnp.int8)
        return 0

    lax.fori_loop(nkt, T // TQ, zero_tile, 0)


def _select(xq, xk, small, B, T, TQ):
    nq = T // TQ
    top_k = min(TOPK_MAX, T // 4)
    kern = functools.partial(_select_kernel, TQ=TQ, T=T, top_k=top_k)
    return pl.pallas_call(
        kern,
        grid=(B, nq),
        in_specs=[pl.BlockSpec((TQ, IDX_WIDTH), lambda b, q: (b * nq + q, 0)),
                  pl.BlockSpec((T, LANES), lambda b, q: (b, 0)),
                  pl.BlockSpec((TQ, LANES), lambda b, q: (b * nq + q, 0))],
        out_specs=pl.BlockSpec((TQ, T), lambda b, q: (b * nq + q, 0)),
        out_shape=jax.ShapeDtypeStruct((B * T, T), jnp.int8),
        scratch_shapes=[pltpu.VMEM((TQ, T), F32), pltpu.VMEM((IDX_HEADS, TQ, LANES), F32)]
                       + [pltpu.VMEM((TQ, LANES), F32)] * 6,
        compiler_params=_cparams(("parallel", "parallel")),
        name="select",
    )(xq, xk, small)


def _attn_kernel(q_ref, k_ref, v_ref, mask_ref, g_ref, out_ref, bias_ref, m_ref, l_ref, acc_ref, *, TQ, TK):
    qi = pl.program_id(1)
    kt = pl.program_id(2)
    nk = pl.num_programs(2)
    last = ((qi + 1) * TQ - 1) // TK

    @pl.when(kt == 0)
    def _():
        m_ref[...] = jnp.full_like(m_ref, NEG_BIG)
        l_ref[...] = jnp.zeros_like(l_ref)
        acc_ref[...] = jnp.zeros_like(acc_ref)

    @pl.when(kt <= last)
    def _():
        bias_ref[...] = (mask_ref[...].astype(F32) - 1.0) * (-NEG_BIG)
        for h in range(A_HEADS):
            sl = slice(h * A_HEAD_DIM, (h + 1) * A_HEAD_DIM)
            s = lax.dot_general(q_ref[:, sl], k_ref[:, sl], (((1,), (1,)), ((), ())),
                                preferred_element_type=F32) + bias_ref[...]
            m_old = m_ref[h]
            m_new = jnp.maximum(m_old, jnp.broadcast_to(jnp.max(s, axis=1, keepdims=True), (TQ, LANES)))
            alpha = jnp.exp(m_old - m_new)
            p = jnp.exp(s - m_new[:, 0:1])
            l_ref[h] = alpha * l_ref[h] + jnp.broadcast_to(jnp.sum(p, axis=1, keepdims=True), (TQ, LANES))
            acc_ref[:, sl] = alpha * acc_ref[:, sl] + jnp.dot(p.astype(BF16), v_ref[:, sl],
                                                              preferred_element_type=F32)
            m_ref[h] = m_new

    @pl.when(kt == nk - 1)
    def _():
        for h in range(A_HEADS):
            sl = slice(h * A_HEAD_DIM, (h + 1) * A_HEAD_DIM)
            out_ref[:, sl] = (acc_ref[:, sl] / l_ref[h] * g_ref[:, sl].astype(F32)).astype(BF16)


def _attn(aq, ak, av, mask, ga_sig, B, T, TQ, TK):
    nq, nk = T // TQ, T // TK

    def kv_map(b, q, k):
        return (b * nk + jnp.minimum(k, ((q + 1) * TQ - 1) // TK), 0)

    def mask_map(b, q, k):
        return (b * nq + q, jnp.minimum(k, ((q + 1) * TQ - 1) // TK))

    qmap = lambda b, q, k: (b * nq + q, 0)
    kern = functools.partial(_attn_kernel, TQ=TQ, TK=TK)
    return pl.pallas_call(
        kern,
        grid=(B, nq, nk),
        in_specs=[pl.BlockSpec((TQ, A_WIDTH), qmap),
                  pl.BlockSpec((TK, A_WIDTH), kv_map),
                  pl.BlockSpec((TK, A_WIDTH), kv_map),
                  pl.BlockSpec((TQ, TK), mask_map),
                  pl.BlockSpec((TQ, A_WIDTH), qmap)],
        out_specs=pl.BlockSpec((TQ, A_WIDTH), qmap),
        out_shape=jax.ShapeDtypeStruct((B * T, A_WIDTH), BF16),
        scratch_shapes=[pltpu.VMEM((TQ, TK), F32),
                        pltpu.VMEM((A_HEADS, TQ, LANES), F32), pltpu.VMEM((A_HEADS, TQ, LANES), F32),
                        pltpu.VMEM((TQ, A_WIDTH), F32)],
        compiler_params=_cparams(("parallel", "parallel", "arbitrary")),
        name="attn",
    )(aq, ak, av, mask, ga_sig)


def _out_proj_kernel(ym_ref, ya_ref, h0_ref, wo_ref, g_ref, b_ref, wr_ref, br_ref,
                     h1_ref, gate_ref, idx_ref, rank_ref, cnt_ref, carry_ref, *, tm, alpha):
    i = pl.program_id(0)

    @pl.when(i == 0)
    def _():
        carry_ref[...] = jnp.zeros_like(carry_ref)

    merged = (ym_ref[...].astype(F32) + ya_ref[...].astype(F32)).astype(BF16)
    y = jnp.dot(merged, wo_ref[...], preferred_element_type=F32)
    h1 = _layer_norm(alpha * h0_ref[...] + y, g_ref[...], b_ref[...])
    h1_ref[...] = h1

    logits = jnp.dot(h1, wr_ref[...], precision=lax.Precision.HIGHEST, preferred_element_type=F32) + br_ref[...]
    lane = lax.broadcasted_iota(I32, (tm, LANES), 1)
    vals, hots = [], []
    idx_out = jnp.zeros((tm, LANES), I32)
    work = logits
    for r in range(TOP_K):
        mx = jnp.max(work, axis=1, keepdims=True)
        first = jnp.min(jnp.where(work == mx, lane, LANES), axis=1, keepdims=True)
        hot = lane == first
        vals.append(mx)
        hots.append(hot)
        idx_out = jnp.where(lane == r, first, idx_out)
        work = jnp.where(hot, -jnp.inf, work)
    exps = [jnp.exp(v - vals[0]) for v in vals]
    tot = exps[0] + exps[1] + exps[2] + exps[3]
    gate_out = jnp.zeros((tm, LANES), F32)
    chosen = jnp.zeros((tm, LANES), F32)
    for r in range(TOP_K):
        gate_out = jnp.where(lane == r, exps[r] / tot, gate_out)
        chosen = chosen + jnp.where(hots[r], 1.0, 0.0)

    rr = lax.broadcasted_iota(I32, (tm, tm), 0)
    cc = lax.broadcasted_iota(I32, (tm, tm), 1)
    strict = (rr > cc).astype(BF16)
    before = jnp.dot(strict, chosen.astype(BF16), preferred_element_type=F32) + carry_ref[...]
    rank_out = jnp.zeros((tm, LANES), I32)
    for r in range(TOP_K):
        rk = jnp.sum(jnp.where(hots[r], before, 0.0), axis=1, keepdims=True)
        rank_out = jnp.where(lane == r, rk.astype(I32), rank_out)
    carry_ref[...] = carry_ref[...] + jnp.sum(chosen, axis=0, keepdims=True)

    gate_ref[...] = gate_out
    idx_ref[...] = idx_out
    rank_ref[...] = rank_out
    cnt_ref[...] = carry_ref[...]


def _out_proj(ym, ya, h0, w_out_b, ln_g, ln_b, w_router, b_router, tm, alpha):
    N = ym.shape[0]
    wr = jnp.zeros((D_MODEL, LANES), F32).at[:, :N_EXPERTS].set(w_router)
    br = jnp.full((1, LANES), NEG_BIG, F32).at[0, :N_EXPERTS].set(b_router)
    row = lambda i: (i, 0)
    const = lambda i: (0, 0)
    kern = functools.partial(_out_proj_kernel, tm=tm, alpha=alpha)
    return pl.pallas_call(
        kern,
        grid=(N // tm,),
        in_specs=[pl.BlockSpec((tm, D_MODEL), row), pl.BlockSpec((tm, D_MODEL), row),
                  pl.BlockSpec((tm, D_MODEL), row),
                  pl.BlockSpec((D_MODEL, D_MODEL), const),
                  pl.BlockSpec((1, D_MODEL), const), pl.BlockSpec((1, D_MODEL), const),
                  pl.BlockSpec((D_MODEL, LANES), const), pl.BlockSpec((1, LANES), const)],
        out_specs=[pl.BlockSpec((tm, D_MODEL), row), pl.BlockSpec((tm, LANES), row),
                   pl.BlockSpec((tm, LANES), row), pl.BlockSpec((tm, LANES), row),
                   pl.BlockSpec((1, LANES), const)],
        out_shape=[jax.ShapeDtypeStruct((N, D_MODEL), F32), jax.ShapeDtypeStruct((N, LANES), F32),
                   jax.ShapeDtypeStruct((N, LANES), I32), jax.ShapeDtypeStruct((N, LANES), I32),
                   jax.ShapeDtypeStruct((1, LANES), F32)],
        scratch_shapes=[pltpu.VMEM((1, LANES), F32)],
        compiler_params=_cparams(("arbitrary",)),
        name="out_proj",
    )(ym, ya, h0, w_out_b, ln_g[None, :], ln_b[None, :], wr, br)


def _dispatch_kernel(pos_ref, h_ref, xs_in_ref, xs_ref, sem, *, tm):
    del xs_in_ref
    base = pl.program_id(0) * (tm * TOP_K)

    def copy(j):
        t = j // TOP_K
        return pltpu.make_async_copy(h_ref.at[pl.ds(t, 1), :], xs_ref.at[pl.ds(pos_ref[base + j], 1), :], sem)

    def start(j, _):
        copy(j).start()
        return 0

    def wait(j, _):
        copy(j).wait()
        return 0

    lax.fori_loop(0, tm * TOP_K, start, 0)
    lax.fori_loop(0, tm * TOP_K, wait, 0)


def _dispatch(h1, pos_flat, n_rows, tm):
    N = h1.shape[0]
    xs0 = jnp.zeros((n_rows, D_MODEL), F32)
    kern = functools.partial(_dispatch_kernel, tm=tm)
    return pl.pallas_call(
        kern,
        grid_spec=pltpu.PrefetchScalarGridSpec(
            num_scalar_prefetch=1,
            grid=(N // tm,),
            in_specs=[pl.BlockSpec((tm, D_MODEL), lambda i, pos: (i, 0)),
                      pl.BlockSpec(memory_space=pl.ANY)],
            out_specs=pl.BlockSpec(memory_space=pl.ANY),
            scratch_shapes=[pltpu.SemaphoreType.DMA(())]),
        out_shape=jax.ShapeDtypeStruct((n_rows, D_MODEL), F32),
        input_output_aliases={2: 0},
        compiler_params=_cparams(("arbitrary",)),
        name="dispatch",
    )(pos_flat, h1, xs0)


def _expert_kernel(te_ref, nused_ref, xs_ref, wg_ref, wu_ref, wd_ref, bg_ref, bu_ref, bd_ref, ys_ref,
                   wgb, wub, wdb):
    i = pl.program_id(0)
    used = i < nused_ref[0]
    fresh = jnp.logical_or(i == 0, te_ref[i] != te_ref[jnp.maximum(i - 1, 0)])

    @pl.when(jnp.logical_and(used, fresh))
    def _():
        wgb[...] = wg_ref[0].astype(BF16)
        wub[...] = wu_ref[0].astype(BF16)
        wdb[...] = wd_ref[0].astype(BF16)

    @pl.when(used)
    def _():
        x = xs_ref[...].astype(BF16)
        g = jnp.minimum(jnp.dot(x, wgb[...], preferred_element_type=F32) + bg_ref[0], SWIGLU_LIMIT)
        u = jnp.clip(jnp.dot(x, wub[...], preferred_element_type=F32) + bu_ref[0], -SWIGLU_LIMIT, SWIGLU_LIMIT)
        act = (u + 1.0) * g * _sigmoid(SWIGLU_ALPHA * g)
        ys_ref[...] = jnp.dot(act.astype(BF16), wdb[...], preferred_element_type=F32) + bd_ref[0]

    @pl.when(jnp.logical_not(used))
    def _():
        ys_ref[...] = jnp.zeros_like(ys_ref)


def _experts(xs, tile_expert, n_used, w_gate, b_gate, w_up, b_up, w_down, b_down, tr):
    P = xs.shape[0]
    wmap = lambda i, te, nu: (te[i], 0, 0)
    return pl.pallas_call(
        _expert_kernel,
        grid_spec=pltpu.PrefetchScalarGridSpec(
            num_scalar_prefetch=2,
            grid=(P // tr,),
            in_specs=[pl.BlockSpec((tr, D_MODEL), lambda i, te, nu: (i, 0)),
                      pl.BlockSpec((1, D_MODEL, D_FF), wmap), pl.BlockSpec((1, D_MODEL, D_FF), wmap),
                      pl.BlockSpec((1, D_FF, D_MODEL), wmap),
                      pl.BlockSpec((1, 1, D_FF), wmap), pl.BlockSpec((1, 1, D_FF), wmap),
                      pl.BlockSpec((1, 1, D_MODEL), wmap)],
            out_specs=pl.BlockSpec((tr, D_MODEL), lambda i, te, nu: (i, 0)),
            scratch_shapes=[pltpu.VMEM((D_MODEL, D_FF), BF16), pltpu.VMEM((D_MODEL, D_FF), BF16),
                            pltpu.VMEM((D_FF, D_MODEL), BF16)]),
        out_shape=jax.ShapeDtypeStruct((P, D_MODEL), F32),
        compiler_params=_cparams(("arbitrary",)),
        name="experts",
    )(tile_expert, n_used, xs, w_gate, w_up, w_down, b_gate[:, None, :], b_up[:, None, :], b_down[:, None, :])


def _combine_kernel(pos_ref, h1_ref, gate_ref, g_ref, b_ref, ys_ref, out_ref, buf, sem, *, tm, alpha):
    base = pl.program_id(0) * (tm * TOP_K)

    def copy(j):
        t = j // TOP_K
        r = j % TOP_K
        return pltpu.make_async_copy(ys_ref.at[pl.ds(pos_ref[base + j], 1), :], buf.at[r, pl.ds(t, 1), :], sem)

    def start(j, _):
        copy(j).start()
        return 0

    def wait(j, _):
        copy(j).wait()
        return 0

    lax.fori_loop(0, tm * TOP_K, start, 0)
    lax.fori_loop(0, tm * TOP_K, wait, 0)
    gates = gate_ref[...]
    moe = gates[:, 0:1] * buf[0]
    for r in range(1, TOP_K):
        moe = moe + gates[:, r:r + 1] * buf[r]
    out_ref[...] = _layer_norm(alpha * h1_ref[...] + moe, g_ref[...], b_ref[...])


def _combine(h1, gates, ys, pos_flat, ln_g, ln_b, tm, alpha):
    N = h1.shape[0]
    kern = functools.partial(_combine_kernel, tm=tm, alpha=alpha)
    return pl.pallas_call(
        kern,
        grid_spec=pltpu.PrefetchScalarGridSpec(
            num_scalar_prefetch=1,
            grid=(N // tm,),
            in_specs=[pl.BlockSpec((tm, D_MODEL), lambda i, pos: (i, 0)),
                      pl.BlockSpec((tm, LANES), lambda i, pos: (i, 0)),
                      pl.BlockSpec((1, D_MODEL), lambda i, pos: (0, 0)),
                      pl.BlockSpec((1, D_MODEL), lambda i, pos: (0, 0)),
                      pl.BlockSpec(memory_space=pl.ANY)],
            out_specs=pl.BlockSpec((tm, D_MODEL), lambda i, pos: (i, 0)),
            scratch_shapes=[pltpu.VMEM((TOP_K, tm, D_MODEL), F32), pltpu.SemaphoreType.DMA(())]),
        out_shape=jax.ShapeDtypeStruct((N, D_MODEL), F32),
        compiler_params=_cparams(("arbitrary",)),
        name="combine",
    )(pos_flat, h1, gates, ln_g[None, :], ln_b[None, :], ys)


def _pick(n, prefs):
    for p in prefs:
        if n % p == 0:
            return p
    raise ValueError(f"no tile size in {prefs} divides {n}")


def _moe_layout(idx, rank, counts, tr):
    tiles = (counts + tr - 1) // tr
    tile_end = jnp.cumsum(tiles)
    offs = (tile_end - tiles) * tr
    pos = offs[idx] + rank
    n_tiles = (idx.shape[0] * TOP_K) // tr + N_EXPERTS
    tile_expert = jnp.searchsorted(tile_end, jnp.arange(n_tiles, dtype=I32), side="right")
    tile_expert = jnp.minimum(tile_expert, N_EXPERTS - 1).astype(I32)
    return pos.reshape(-1).astype(I32), tile_expert, tile_end[-1:].astype(I32), n_tiles * tr


def kernel(x, ln_in_g, ln_in_b, w_in, b_in, conv_w, conv_b, m_norm_g, w_out, ln1_g, ln1_b, w_router, b_router,
           w_gate, b_gate, w_up, b_up, w_down, b_down, ln2_g, ln2_b):
    B, T, D = x.shape
    depth = w_in.shape[0]
    assert D == D_MODEL and T % 256 == 0 and depth == 1
    alpha = (2.0 * depth) ** 0.25
    N = B * T
    tm_proj = _pick(T, (512, 256))
    L = 256
    TQ_SEL = 256
    TQ_ATT, TK_ATT = 256, _pick(T, (512, 256))
    tm_out = _pick(N, (512, 256))
    tm_tok = _pick(N, (128,))
    tr = 256

    h = x.reshape(N, D)
    for l in range(depth):
        wp, bp = _pack_in_weights(w_in[l], b_in[l])
        (h0, mq, mk, mv, mo_sig, aq, ak, av, xq, gm_sig, ga_sig, xk, small) = _in_proj(
            h, ln_in_g, ln_in_b, wp, bp, T, tm_proj)
        ym = _mlstm(mq, mk, mv, mo_sig, gm_sig, small, conv_w[l], conv_b[l], m_norm_g[l], B, T, L)
        mask = _select(xq, xk, small, B, T, TQ_SEL)
        ya = _attn(aq, ak, av, mask, ga_sig, B, T, TQ_ATT, TK_ATT)
        h1, gates, idx, rank, counts = _out_proj(ym, ya, h0, w_out[l].astype(BF16), ln1_g[l], ln1_b[l],
                                                 w_router[l], b_router[l], tm_out, alpha)
        pos, tile_expert, n_used, n_rows = _moe_layout(idx[:, :TOP_K], rank[:, :TOP_K],
                                                       counts[0, :N_EXPERTS].astype(I32), tr)
        xs = _dispatch(h1, pos, n_rows, tm_tok)
        ys = _experts(xs, tile_expert, n_used, w_gate[l], b_gate[l], w_up[l], b_up[l], w_down[l], b_down[l], tr)
        h = _combine(h1, gates, ys, pos, ln2_g[l], ln2_b[l], tm_tok, alpha)
    return h.reshape(B, T, D)
```

```python
import functools
import math

import jax
import jax.numpy as jnp
from jax import lax
from jax.experimental import pallas as pl
from jax.experimental.pallas import tpu as pltpu

F32 = jnp.float32
BF16 = jnp.bfloat16
I32 = jnp.int32

D_MODEL = 1024
CHUNK = 64
M_HEADS = 4
M_HEAD_DIM = D_MODEL // M_HEADS
M_WIDTH = M_HEADS * M_HEAD_DIM
CONV_WIDTH = 4
A_HEADS = 8
A_HEAD_DIM = D_MODEL // A_HEADS
A_WIDTH = A_HEADS * A_HEAD_DIM
IDX_HEADS = 8
IDX_DIM = 64
IDX_WIDTH = IDX_HEADS * IDX_DIM
TOPK_MAX = 256
ROPE_THETA = 10000.0
N_EXPERTS = 32
TOP_K = 4
D_FF = D_MODEL
SWIGLU_LIMIT = 7.0
SWIGLU_ALPHA = 1.702
LN_EPS = 1e-5

LANES = 128
NEG_BIG = -1e30
F32_TINY = 1.1754943508222875e-38
VMEM_LIMIT = 56 * 1024 * 1024

_BIG_GROUPS = ("m_q", "m_k", "m_v", "m_o", "a_q", "a_k", "a_v", "x_q", "g_m", "g_a")
_GROUP_WIDTH = dict(m_q=M_WIDTH, m_k=M_WIDTH, m_v=M_WIDTH, m_o=M_WIDTH, a_q=A_WIDTH, a_k=A_WIDTH,
                    a_v=A_WIDTH, x_q=IDX_WIDTH, g_m=D_MODEL, g_a=D_MODEL, x_k=LANES, small=LANES)
_GROUP_ORDER = _BIG_GROUPS + ("x_k", "small")
_GROUP_START = {}
_off = 0
for _g in _GROUP_ORDER:
    _GROUP_START[_g] = _off
    _off += _GROUP_WIDTH[_g]
PACKED_WIDTH = _off


def _cparams(sem, vmem=VMEM_LIMIT):
    return pltpu.CompilerParams(dimension_semantics=sem, vmem_limit_bytes=vmem)


def _layer_norm(x, g, b):
    mu = jnp.mean(x, axis=-1, keepdims=True)
    xc = x - mu
    var = jnp.mean(xc * xc, axis=-1, keepdims=True)
    return xc * lax.rsqrt(var + LN_EPS) * g + b


def _sigmoid(x):
    return 1.0 / (1.0 + jnp.exp(-x))


def _in_proj_kernel(x_ref, g_ref, b_ref, w_ref, bias_ref, cosa_ref, sina_ref, cosb_ref, sinlo_ref, sinhi_ref,
                    h0_ref, mq_ref, mk_ref, mv_ref, mo_ref, aq_ref, ak_ref, av_ref, xq_ref, gm_ref, ga_ref,
                    xk_ref, small_ref):
    h0 = _layer_norm(x_ref[...], g_ref[...], b_ref[...])
    h0_ref[...] = h0
    hb = h0.astype(BF16)

    def proj(name, c0=0, width=None):
        start = _GROUP_START[name] + c0
        width = _GROUP_WIDTH[name] if width is None else width
        y = jnp.dot(hb, w_ref[:, start:start + width], preferred_element_type=F32)
        return y + bias_ref[:, start:start + width]

    def rope_full(y):
        return y * cosa_ref[...] + pltpu.roll(y, 64, axis=1) * sina_ref[...]

    def rope_half(y):
        return (y * cosb_ref[...] + pltpu.roll(y, 96, axis=1) * sinlo_ref[...]
                + pltpu.roll(y, 32, axis=1) * sinhi_ref[...])

    mq_ref[...] = proj("m_q").astype(BF16)
    mk_ref[...] = proj("m_k").astype(BF16)
    mv_ref[...] = proj("m_v").astype(BF16)
    mo_ref[...] = _sigmoid(proj("m_o")).astype(BF16)
    gm_ref[...] = _sigmoid(proj("g_m")).astype(BF16)
    ga_ref[...] = _sigmoid(proj("g_a")).astype(BF16)
    av_ref[...] = proj("a_v").astype(BF16)
    q_scale = A_HEAD_DIM ** -0.5
    for h in range(A_HEADS):
        sl = slice(h * LANES, (h + 1) * LANES)
        aq_ref[:, sl] = (rope_full(proj("a_q", h * LANES, LANES)) * q_scale).astype(BF16)
        ak_ref[:, sl] = rope_full(proj("a_k", h * LANES, LANES)).astype(BF16)
    for c in range(IDX_WIDTH // LANES):
        sl = slice(c * LANES, (c + 1) * LANES)
        xq_ref[:, sl] = rope_half(proj("x_q", c * LANES, LANES)).astype(BF16)
    xk_ref[...] = rope_half(proj("x_k")).astype(BF16)
    small_ref[...] = proj("small")


def _rope_tables(T):
    pos = jnp.arange(T, dtype=F32)[:, None]
    half_a = A_HEAD_DIM // 2
    inv_a = ROPE_THETA ** (-jnp.arange(half_a, dtype=F32) / half_a)
    ang_a = pos * inv_a[None, :]
    cos_a, sin_a = jnp.cos(ang_a), jnp.sin(ang_a)
    cosa = jnp.concatenate([cos_a, cos_a], axis=1)
    sina = jnp.concatenate([-sin_a, sin_a], axis=1)
    half_b = IDX_DIM // 2
    inv_b = ROPE_THETA ** (-jnp.arange(half_b, dtype=F32) / half_b)
    ang_b = pos * inv_b[None, :]
    cos_b, sin_b = jnp.cos(ang_b), jnp.sin(ang_b)
    zero = jnp.zeros_like(sin_b)
    cosb = jnp.concatenate([cos_b, cos_b, cos_b, cos_b], axis=1)
    sinlo = jnp.concatenate([-sin_b, zero, -sin_b, zero], axis=1)
    sinhi = jnp.concatenate([zero, sin_b, zero, sin_b], axis=1)
    return cosa, sina, cosb, sinlo, sinhi


def _pack_in_weights(w, b):
    sizes = (M_WIDTH, M_WIDTH, M_WIDTH, M_WIDTH, M_HEADS, M_HEADS, A_WIDTH, A_WIDTH, A_WIDTH,
             IDX_WIDTH, IDX_DIM, IDX_HEADS, D_MODEL, D_MODEL)
    names = ("m_q", "m_k", "m_v", "m_o", "m_i", "m_f", "a_q", "a_k", "a_v", "x_q", "x_k", "x_w", "g_m", "g_a")
    parts, start = {}, 0
    for n, s in zip(names, sizes):
        parts[n] = (w[:, start:start + s], b[start:start + s])
        start += s
    rows = w.shape[0]

    def pad(n_cols):
        return jnp.zeros((rows, n_cols), w.dtype), jnp.zeros((n_cols,), b.dtype)

    order = [parts[n] for n in _BIG_GROUPS]
    order += [parts["x_k"], pad(LANES - IDX_DIM)]
    order += [parts["m_i"], parts["m_f"], parts["x_w"], pad(LANES - 2 * M_HEADS - IDX_HEADS)]
    wp = jnp.concatenate([o[0] for o in order], axis=1).astype(BF16)
    bp = jnp.concatenate([o[1] for o in order], axis=0).astype(F32)[None, :]
    return wp, bp


def _in_proj(x2, ln_g, ln_b, wp, bp, T, tm):
    N = x2.shape[0]
    nt = T // tm
    tables = _rope_tables(T)
    row = lambda i: (i, 0)
    const = lambda i: (0, 0)
    tab = lambda i: (i % nt, 0)
    widths = [D_MODEL, M_WIDTH, M_WIDTH, M_WIDTH, M_WIDTH, A_WIDTH, A_WIDTH, A_WIDTH, IDX_WIDTH, D_MODEL, D_MODEL,
              LANES, LANES]
    dtypes = [F32] + [BF16] * 11 + [F32]
    return pl.pallas_call(
        _in_proj_kernel,
        grid=(N // tm,),
        in_specs=[pl.BlockSpec((tm, D_MODEL), row),
                  pl.BlockSpec((1, D_MODEL), const), pl.BlockSpec((1, D_MODEL), const),
                  pl.BlockSpec((D_MODEL, PACKED_WIDTH), const, pipeline_mode=pl.Buffered(1)),
                  pl.BlockSpec((1, PACKED_WIDTH), const)]
                 + [pl.BlockSpec((tm, LANES), tab)] * 5,
        out_specs=[pl.BlockSpec((tm, wd), row) for wd in widths],
        out_shape=[jax.ShapeDtypeStruct((N, wd), dt) for wd, dt in zip(widths, dtypes)],
        compiler_params=_cparams(("parallel",)),
        name="in_proj",
    )(x2, ln_g[None, :], ln_b[None, :], wp, bp, *tables)


def _log_sigmoid(x):
    return jnp.minimum(x, 0.0) - jnp.log(1.0 + jnp.exp(-jnp.abs(x)))


def _mlstm_kernel(q_ref, k_ref, v_ref, o_ref, gm_ref, gcol_ref, grow_ref, cwq_ref, cwk_ref, cbq_ref, cbk_ref,
                  ng_ref, out_ref, qext, kext, c_state, n_state, m_state, *, L):
    c = pl.program_id(2)
    halo = 8

    @pl.when(c == 0)
    def _():
        qext[0:halo, :] = jnp.zeros((halo, M_HEAD_DIM), F32)
        kext[0:halo, :] = jnp.zeros((halo, M_HEAD_DIM), F32)
        c_state[...] = jnp.zeros_like(c_state)
        n_state[...] = jnp.zeros_like(n_state)
        m_state[...] = jnp.zeros_like(m_state)

    def conv_silu(ext, x_ref, w_ref, b_ref):
        x = x_ref[...].astype(F32)
        ext[halo:halo + L, :] = x
        acc = b_ref[...] + ext[pl.ds(halo - 3, L), :] * w_ref[0:1, :]
        for j in range(1, CONV_WIDTH):
            acc = acc + ext[pl.ds(halo - 3 + j, L), :] * w_ref[j:j + 1, :]
        ext[0:halo, :] = x[L - halo:L, :]
        return acc * _sigmoid(acc)

    qc = conv_silu(qext, q_ref, cwq_ref, cbq_ref) * (M_HEAD_DIM ** -0.5)
    kc = conv_silu(kext, k_ref, cwk_ref, cbk_ref)
    v = v_ref[...]

    gcol = gcol_ref[0, 0]
    grow = grow_ref[0, 0]
    i_col, lf_col = gcol[:, 0:1], _log_sigmoid(gcol[:, 1:2])
    i_row, lf_row = grow[0:1, :], _log_sigmoid(grow[1:2, :])

    r = lax.broadcasted_iota(I32, (L, L), 0)
    s = lax.broadcasted_iota(I32, (L, L), 1)
    causal = r >= s
    hi = lax.Precision.HIGHEST
    b_col = jnp.dot(causal.astype(F32), jnp.broadcast_to(lf_col, (L, LANES)), precision=hi,
                    preferred_element_type=F32)[:, 0:1]
    b_row = jnp.dot(jnp.broadcast_to(lf_row, (8, L)), (r <= s).astype(F32), precision=hi,
                    preferred_element_type=F32)[0:1, :]

    m_prev = m_state[...]
    dm = jnp.where(causal, b_col - b_row + i_row, -jnp.inf)
    m_inter = b_col + m_prev
    m_t = jnp.maximum(m_inter, jnp.max(dm, axis=1, keepdims=True))
    qb, kb = qc.astype(BF16), kc.astype(BF16)
    qk = lax.dot_general(qb, kb, (((1,), (1,)), ((), ())), preferred_element_type=F32)
    sm = qk * jnp.exp(dm - m_t)
    w_inter = jnp.exp(m_inter - m_t)
    cb = c_state[...].astype(BF16)
    num = (jnp.dot(sm.astype(BF16), v, preferred_element_type=F32)
           + w_inter * jnp.dot(qb, cb, preferred_element_type=F32))
    den = (jnp.sum(sm, axis=1, keepdims=True)
           + w_inter * jnp.sum(qc * n_state[...], axis=1, keepdims=True))
    hh = num / jnp.maximum(jnp.abs(den), jnp.exp(-m_t))

    b_last = b_col[L - 1:L, :]
    g_row = b_last - b_row + i_row
    m_new = jnp.maximum(b_last + m_prev, jnp.max(g_row, axis=1, keepdims=True))
    wg_col = jnp.exp(b_last - b_col + i_col - m_new)
    decay = jnp.exp(b_last + m_prev - m_new)
    kw = kc * wg_col
    kv = lax.dot_general(kw.astype(BF16), v, (((0,), (0,)), ((), ())), preferred_element_type=F32)
    c_state[...] = decay * c_state[...] + kv
    n_state[...] = decay * n_state[...] + jnp.sum(kw, axis=0, keepdims=True)
    m_state[...] = m_new

    mu = jnp.mean(hh, axis=1, keepdims=True)
    hc = hh - mu
    var = jnp.mean(hc * hc, axis=1, keepdims=True)
    hn = hc * lax.rsqrt(var + LN_EPS) * ng_ref[...]
    out_ref[...] = (hn * o_ref[...].astype(F32) * gm_ref[...].astype(F32)).astype(BF16)


def _mlstm(mq, mk, mv, mo_sig, gm_sig, small, conv_w, conv_b, norm_g, B, T, L):
    nL = T // L
    i_pre = small[:, 0:M_HEADS].reshape(B, T, M_HEADS)
    f_pre = small[:, M_HEADS:2 * M_HEADS].reshape(B, T, M_HEADS)
    gates = jnp.stack([i_pre, f_pre], axis=-1)
    gcol = gates.transpose(0, 2, 1, 3)
    grow = gates.transpose(0, 2, 3, 1)
    blk = lambda b, h, c: (b * nL + c, h)
    head = lambda b, h, c: (0, h)
    kern = functools.partial(_mlstm_kernel, L=L)
    return pl.pallas_call(
        kern,
        grid=(B, M_HEADS, nL),
        in_specs=[pl.BlockSpec((L, M_HEAD_DIM), blk)] * 5
                 + [pl.BlockSpec((1, 1, L, 2), lambda b, h, c: (b, h, c, 0)),
                    pl.BlockSpec((1, 1, 2, L), lambda b, h, c: (b, h, 0, c)),
                    pl.BlockSpec((CONV_WIDTH, M_HEAD_DIM), head),
                    pl.BlockSpec((CONV_WIDTH, M_HEAD_DIM), head),
                    pl.BlockSpec((1, M_HEAD_DIM), head),
                    pl.BlockSpec((1, M_HEAD_DIM), head),
                    pl.BlockSpec((1, M_HEAD_DIM), head)],
        out_specs=pl.BlockSpec((L, M_HEAD_DIM), blk),
        out_shape=jax.ShapeDtypeStruct((B * T, M_WIDTH), BF16),
        scratch_shapes=[pltpu.VMEM((L + 8, M_HEAD_DIM), F32), pltpu.VMEM((L + 8, M_HEAD_DIM), F32),
                        pltpu.VMEM((M_HEAD_DIM, M_HEAD_DIM), F32), pltpu.VMEM((1, M_HEAD_DIM), F32),
                        pltpu.VMEM((1, 1), F32)],
        compiler_params=_cparams(("parallel", "parallel", "arbitrary")),
        name="mlstm",
    )(mq, mk, mv, mo_sig, gm_sig, gcol, grow,
      conv_w[:, :M_WIDTH], conv_w[:, M_WIDTH:], conv_b[None, :M_WIDTH], conv_b[None, M_WIDTH:], norm_g[None, :])


def _select_kernel(xq_ref, xk_ref, small_ref, mask_ref, sc_ref, w_ref, lo_ref, hi_ref, tie_ref, *, TQ, T, top_k):
    qi = pl.program_id(1)
    nkt = qi + 1
    PW = 2 * TQ
    npair = (nkt + 1) // 2
    RB = 128
    GB = 32
    kf = float(top_k)
    w_scale = (IDX_DIM ** -0.5) * (IDX_HEADS ** -0.5)

    wsm = small_ref[...]
    for h in range(IDX_HEADS):
        col = 2 * M_HEADS + h
        w_ref[h] = jnp.broadcast_to(wsm[:, col:col + 1] * w_scale, (TQ, LANES))

    row_chunk = lax.broadcasted_iota(I32, (TQ, TQ), 0) // CHUNK
    col_chunk = lax.broadcasted_iota(I32, (TQ, TQ), 1) // CHUNK
    diag_ok = col_chunk <= row_chunk

    def score_tile(kt, carry):
        mx, mn = carry
        koff = pl.multiple_of(kt * TQ, TQ)
        kblk = xk_ref[pl.ds(koff, TQ), 0:IDX_DIM]
        acc = jnp.zeros((TQ, TQ), F32)
        for h in range(IDX_HEADS):
            qh = xq_ref[:, h * IDX_DIM:(h + 1) * IDX_DIM]
            sh = lax.dot_general(qh, kblk, (((1,), (1,)), ((), ())), preferred_element_type=F32)
            wh = w_ref[h]
            acc = acc + jnp.maximum(sh, 0.0) * jnp.concatenate([wh] * (TQ // LANES), axis=1)
        ok = jnp.logical_or(kt < qi, diag_ok)
        sc_ref[:, pl.ds(koff, TQ)] = jnp.where(ok, acc, NEG_BIG)
        mx = jnp.maximum(mx, jnp.max(jnp.where(ok, acc, NEG_BIG), axis=1, keepdims=True))
        mn = jnp.minimum(mn, jnp.min(jnp.where(ok, acc, -NEG_BIG), axis=1, keepdims=True))
        return mx, mn

    mx, mn = lax.fori_loop(0, nkt, score_tile,
                           (jnp.full((TQ, 1), NEG_BIG, F32), jnp.full((TQ, 1), -NEG_BIG, F32)))

    @pl.when(nkt % 2 == 1)
    def _():
        sc_ref[:, pl.ds(pl.multiple_of(nkt * TQ, TQ), TQ)] = jnp.full((TQ, TQ), NEG_BIG, F32)

    lo_ref[...] = jnp.broadcast_to(mn, (TQ, LANES))
    hi_ref[...] = jnp.broadcast_to(mx + jnp.maximum(jnp.abs(mx), 1e-30) * 1e-3, (TQ, LANES))

    def any_rows(flag):
        return jnp.max(jnp.where(flag, 1.0, 0.0)) > 0.0

    def count(rows, nrows, pred):
        def body(p, cnt):
            koff = pl.multiple_of(p * PW, PW)
            for cc in range(PW // LANES):
                sv = sc_ref[rows, pl.ds(koff + cc * LANES, LANES)]
                cnt = cnt + jnp.where(pred(sv, koff + cc * LANES), 1.0, 0.0)
            return cnt

        cnt = lax.fori_loop(0, npair, body, jnp.zeros((nrows, LANES), F32))
        return jnp.broadcast_to(jnp.sum(cnt, axis=1, keepdims=True), (nrows, LANES))

    def write_mask(rows, keep):
        def write(p, _):
            koff = pl.multiple_of(p * PW, PW)
            for cc in range(PW // LANES):
                sv = sc_ref[rows, pl.ds(koff + cc * LANES, LANES)]
                mask_ref[rows, pl.ds(koff + cc * LANES, LANES)] = jnp.where(
                    keep(sv, koff + cc * LANES), 1.0, 0.0).astype(jnp.int8)
            return 0

        lax.fori_loop(0, npair, write, 0)

    def row_block(rb, _):
        r0 = pl.multiple_of(rb * RB, RB)
        rows = pl.ds(r0, RB)
        row_id = qi * TQ + r0 + lax.broadcasted_iota(I32, (RB, LANES), 0)
        n_adm = ((row_id // CHUNK + 1) * CHUNK).astype(F32)
        lo0, hi0 = lo_ref[rows, :], hi_ref[rows, :]

        c_zero = count(rows, RB, lambda sv, off: sv >= 0.0)
        c_pos = count(rows, RB, lambda sv, off: sv >= F32_TINY)
        pos_side = jnp.logical_and(c_pos >= kf, lo0 < F32_TINY)
        zero_hit = jnp.logical_and(c_pos < kf, c_zero >= kf)
        neg_side = jnp.logical_and(c_zero < kf, hi0 > 0.0)
        lo1 = jnp.where(pos_side, F32_TINY, jnp.where(zero_hit, 0.0, lo0))
        clo1 = jnp.where(pos_side, c_pos, jnp.where(zero_hit, c_zero, n_adm))
        hi1 = jnp.where(zero_hit, F32_TINY, jnp.where(neg_side, 0.0, hi0))

        def active(lo, hi, clo):
            mid = 0.5 * lo + 0.5 * hi
            return jnp.logical_and(clo > kf, jnp.logical_and(mid > lo, mid < hi)), mid

        def cond(carry):
            return jnp.logical_and(carry[1], carry[0] < 400)

        def body(carry):
            it, _, lo, hi, clo = carry
            act, mid = active(lo, hi, clo)
            cnt = count(rows, RB, lambda sv, off: sv >= mid)
            ge = cnt >= kf
            up = jnp.logical_and(act, ge)
            lo = jnp.where(up, mid, lo)
            clo = jnp.where(up, cnt, clo)
            hi = jnp.where(jnp.logical_and(act, jnp.logical_not(ge)), mid, hi)
            act2, _ = active(lo, hi, clo)
            return it + 1, any_rows(act2), lo, hi, clo

        act1, _ = active(lo1, hi1, clo1)
        _, _, lo, hi, clo = lax.while_loop(cond, body, (jnp.int32(0), any_rows(act1), lo1, hi1, clo1))
        lo_ref[rows, :] = lo
        hi_ref[rows, :] = hi
        tie = clo > kf
        tie_ref[rows, :] = jnp.where(tie, 1.0, 0.0)
        write_mask(rows, lambda sv, off: sv >= lo_ref[rows, :])

        @pl.when(any_rows(tie))
        def _():
            def tie_group(g, _):
                grows = pl.ds(pl.multiple_of(r0 + g * GB, GB), GB)
                tied = tie_ref[grows, :] > 0.5

                @pl.when(any_rows(tied))
                def _():
                    glo, ghi = lo_ref[grows, :], hi_ref[grows, :]

                    def key_index(off):
                        return (off + lax.broadcasted_iota(I32, (GB, LANES), 1)).astype(F32)

                    need = kf - count(grows, GB, lambda sv, off: sv >= ghi)
                    jlo = jnp.zeros((GB, LANES), F32)
                    jhi = jnp.full((GB, LANES), float(T), F32)
                    for _i in range(int(math.log2(T)) + 1):
                        jmid = jnp.floor(0.5 * (jlo + jhi))
                        cnt = count(grows, GB, lambda sv, off: jnp.logical_and(
                            jnp.logical_and(sv >= glo, sv < ghi), key_index(off) < jmid))
                        ge = cnt >= need
                        jhi = jnp.where(ge, jmid, jhi)
                        jlo = jnp.where(ge, jlo, jmid)
                    jsel = jnp.where(tied, jhi, float(T))
                    write_mask(grows, lambda sv, off: jnp.logical_and(
                        sv >= glo, jnp.logical_or(sv >= ghi, key_index(off) < jsel)))

                return 0

            lax.fori_loop(0, RB // GB, tie_group, 0)

        return 0

    lax.fori_loop(0, TQ // RB, row_block, 0)

    def zero_pair(p, _):
        koff = pl.multiple_of(p * PW, PW)
        mask_ref[:, pl.ds(koff, PW)] = jnp.zeros((TQ, PW), jnp.int8)
        return 0

    lax.fori_loop(npair, T // PW, zero_pair, 0)


def _select(xq, xk, small, B, T, TQ):
    assert T % (2 * TQ) == 0
    nq = T // TQ
    top_k = min(TOPK_MAX, T // 4)
    kern = functools.partial(_select_kernel, TQ=TQ, T=T, top_k=top_k)
    return pl.pallas_call(
        kern,
        grid=(B, nq),
        in_specs=[pl.BlockSpec((TQ, IDX_WIDTH), lambda b, q: (b * nq + q, 0)),
                  pl.BlockSpec((T, LANES), lambda b, q: (b, 0)),
                  pl.BlockSpec((TQ, LANES), lambda b, q: (b * nq + q, 0))],
        out_specs=pl.BlockSpec((TQ, T), lambda b, q: (b * nq + q, 0)),
        out_shape=jax.ShapeDtypeStruct((B * T, T), jnp.int8),
        scratch_shapes=[pltpu.VMEM((TQ, T), F32), pltpu.VMEM((IDX_HEADS, TQ, LANES), F32),
                        pltpu.VMEM((TQ, LANES), F32), pltpu.VMEM((TQ, LANES), F32),
                        pltpu.VMEM((TQ, LANES), F32)],
        compiler_params=_cparams(("parallel", "parallel")),
        name="select",
    )(xq, xk, small)


def _attn_kernel(q_ref, k_ref, v_ref, mask_ref, g_ref, out_ref, bias_ref, m_ref, l_ref, acc_ref, *, TQ, TK):
    qi = pl.program_id(1)
    kt = pl.program_id(2)
    nk = pl.num_programs(2)
    last = ((qi + 1) * TQ - 1) // TK

    @pl.when(kt == 0)
    def _():
        m_ref[...] = jnp.full_like(m_ref, NEG_BIG)
        l_ref[...] = jnp.zeros_like(l_ref)
        acc_ref[...] = jnp.zeros_like(acc_ref)

    @pl.when(kt <= last)
    def _():
        bias_ref[...] = (mask_ref[...].astype(F32) - 1.0) * (-NEG_BIG)
        for h in range(A_HEADS):
            sl = slice(h * A_HEAD_DIM, (h + 1) * A_HEAD_DIM)
            s = lax.dot_general(q_ref[:, sl], k_ref[:, sl], (((1,), (1,)), ((), ())),
                                preferred_element_type=F32) + bias_ref[...]
            m_old = m_ref[h]
            m_new = jnp.maximum(m_old, jnp.broadcast_to(jnp.max(s, axis=1, keepdims=True), (TQ, LANES)))
            alpha = jnp.exp(m_old - m_new)
            p = jnp.exp(s - m_new[:, 0:1])
            l_ref[h] = alpha * l_ref[h] + jnp.broadcast_to(jnp.sum(p, axis=1, keepdims=True), (TQ, LANES))
            acc_ref[:, sl] = alpha * acc_ref[:, sl] + jnp.dot(p.astype(BF16), v_ref[:, sl],
                                                              preferred_element_type=F32)
            m_ref[h] = m_new

    @pl.when(kt == nk - 1)
    def _():
        for h in range(A_HEADS):
            sl = slice(h * A_HEAD_DIM, (h + 1) * A_HEAD_DIM)
            out_ref[:, sl] = (acc_ref[:, sl] / l_ref[h] * g_ref[:, sl].astype(F32)).astype(BF16)


def _attn(aq, ak, av, mask, ga_sig, B, T, TQ, TK):
    nq, nk = T // TQ, T // TK

    def kv_map(b, q, k):
        return (b * nk + jnp.minimum(k, ((q + 1) * TQ - 1) // TK), 0)

    def mask_map(b, q, k):
        return (b * nq + q, jnp.minimum(k, ((q + 1) * TQ - 1) // TK))

    qmap = lambda b, q, k: (b * nq + q, 0)
    kern = functools.partial(_attn_kernel, TQ=TQ, TK=TK)
    return pl.pallas_call(
        kern,
        grid=(B, nq, nk),
        in_specs=[pl.BlockSpec((TQ, A_WIDTH), qmap),
                  pl.BlockSpec((TK, A_WIDTH), kv_map),
                  pl.BlockSpec((TK, A_WIDTH), kv_map),
                  pl.BlockSpec((TQ, TK), mask_map),
                  pl.BlockSpec((TQ, A_WIDTH), qmap)],
        out_specs=pl.BlockSpec((TQ, A_WIDTH), qmap),
        out_shape=jax.ShapeDtypeStruct((B * T, A_WIDTH), BF16),
        scratch_shapes=[pltpu.VMEM((TQ, TK), F32),
                        pltpu.VMEM((A_HEADS, TQ, LANES), F32), pltpu.VMEM((A_HEADS, TQ, LANES), F32),
                        pltpu.VMEM((TQ, A_WIDTH), F32)],
        compiler_params=_cparams(("parallel", "parallel", "arbitrary")),
        name="attn",
    )(aq, ak, av, mask, ga_sig)


def _out_proj_kernel(ym_ref, ya_ref, h0_ref, wo_ref, g_ref, b_ref, wr_ref, br_ref,
                     h1_ref, gate_ref, idx_ref, rank_ref, cnt_ref, carry_ref, *, tm, alpha):
    i = pl.program_id(0)

    @pl.when(i == 0)
    def _():
        carry_ref[...] = jnp.zeros_like(carry_ref)

    merged = (ym_ref[...].astype(F32) + ya_ref[...].astype(F32)).astype(BF16)
    y = jnp.dot(merged, wo_ref[...], preferred_element_type=F32)
    h1 = _layer_norm(alpha * h0_ref[...] + y, g_ref[...], b_ref[...])
    h1_ref[...] = h1

    logits = jnp.dot(h1, wr_ref[...], precision=lax.Precision.HIGHEST, preferred_element_type=F32) + br_ref[...]
    lane = lax.broadcasted_iota(I32, (tm, LANES), 1)
    vals, hots = [], []
    idx_out = jnp.zeros((tm, LANES), I32)
    work = logits
    for r in range(TOP_K):
        mx = jnp.max(work, axis=1, keepdims=True)
        first = jnp.min(jnp.where(work == mx, lane, LANES), axis=1, keepdims=True)
        hot = lane == first
        vals.append(mx)
        hots.append(hot)
        idx_out = jnp.where(lane == r, first, idx_out)
        work = jnp.where(hot, -jnp.inf, work)
    exps = [jnp.exp(v - vals[0]) for v in vals]
    tot = exps[0] + exps[1] + exps[2] + exps[3]
    gate_out = jnp.zeros((tm, LANES), F32)
    chosen = jnp.zeros((tm, LANES), F32)
    for r in range(TOP_K):
        gate_out = jnp.where(lane == r, exps[r] / tot, gate_out)
        chosen = chosen + jnp.where(hots[r], 1.0, 0.0)

    rr = lax.broadcasted_iota(I32, (tm, tm), 0)
    cc = lax.broadcasted_iota(I32, (tm, tm), 1)
    strict = (rr > cc).astype(BF16)
    before = jnp.dot(strict, chosen.astype(BF16), preferred_element_type=F32) + carry_ref[...]
    rank_out = jnp.zeros((tm, LANES), I32)
    for r in range(TOP_K):
        rk = jnp.sum(jnp.where(hots[r], before, 0.0), axis=1, keepdims=True)
        rank_out = jnp.where(lane == r, rk.astype(I32), rank_out)
    carry_ref[...] = carry_ref[...] + jnp.sum(chosen, axis=0, keepdims=True)

    gate_ref[...] = gate_out
    idx_ref[...] = idx_out
    rank_ref[...] = rank_out
    cnt_ref[...] = carry_ref[...]


def _out_proj(ym, ya, h0, w_out_b, ln_g, ln_b, w_router, b_router, tm, alpha):
    N = ym.shape[0]
    wr = jnp.zeros((D_MODEL, LANES), F32).at[:, :N_EXPERTS].set(w_router)
    br = jnp.full((1, LANES), NEG_BIG, F32).at[0, :N_EXPERTS].set(b_router)
    row = lambda i: (i, 0)
    const = lambda i: (0, 0)
    kern = functools.partial(_out_proj_kernel, tm=tm, alpha=alpha)
    return pl.pallas_call(
        kern,
        grid=(N // tm,),
        in_specs=[pl.BlockSpec((tm, D_MODEL), row), pl.BlockSpec((tm, D_MODEL), row),
                  pl.BlockSpec((tm, D_MODEL), row),
                  pl.BlockSpec((D_MODEL, D_MODEL), const),
                  pl.BlockSpec((1, D_MODEL), const), pl.BlockSpec((1, D_MODEL), const),
                  pl.BlockSpec((D_MODEL, LANES), const), pl.BlockSpec((1, LANES), const)],
        out_specs=[pl.BlockSpec((tm, D_MODEL), row), pl.BlockSpec((tm, LANES), row),
                   pl.BlockSpec((tm, LANES), row), pl.BlockSpec((tm, LANES), row),
                   pl.BlockSpec((1, LANES), const)],
        out_shape=[jax.ShapeDtypeStruct((N, D_MODEL), F32), jax.ShapeDtypeStruct((N, LANES), F32),
                   jax.ShapeDtypeStruct((N, LANES), I32), jax.ShapeDtypeStruct((N, LANES), I32),
                   jax.ShapeDtypeStruct((1, LANES), F32)],
        scratch_shapes=[pltpu.VMEM((1, LANES), F32)],
        compiler_params=_cparams(("arbitrary",)),
        name="out_proj",
    )(ym, ya, h0, w_out_b, ln_g[None, :], ln_b[None, :], wr, br)


def _dispatch_kernel(pos_ref, h_ref, xs_in_ref, xs_ref, sem, *, tm):
    del xs_in_ref
    base = pl.program_id(0) * (tm * TOP_K)

    def copy(t, r):
        dst = pos_ref[base + t * TOP_K + r]
        return pltpu.make_async_copy(h_ref.at[pl.ds(t, 1), :], xs_ref.at[pl.ds(dst, 1), :], sem)

    def start(t, _):
        for r in range(TOP_K):
            copy(t, r).start()
        return 0

    def wait(t, _):
        for r in range(TOP_K):
            copy(t, r).wait()
        return 0

    lax.fori_loop(0, tm, start, 0)
    lax.fori_loop(0, tm, wait, 0)


def _dispatch(h1, pos_flat, n_rows, tm):
    N = h1.shape[0]
    xs0 = jnp.zeros((n_rows, D_MODEL), F32)
    kern = functools.partial(_dispatch_kernel, tm=tm)
    return pl.pallas_call(
        kern,
        grid_spec=pltpu.PrefetchScalarGridSpec(
            num_scalar_prefetch=1,
            grid=(N // tm,),
            in_specs=[pl.BlockSpec((tm, D_MODEL), lambda i, pos: (i, 0)),
                      pl.BlockSpec(memory_space=pl.ANY)],
            out_specs=pl.BlockSpec(memory_space=pl.ANY),
            scratch_shapes=[pltpu.SemaphoreType.DMA(())]),
        out_shape=jax.ShapeDtypeStruct((n_rows, D_MODEL), F32),
        input_output_aliases={2: 0},
        compiler_params=_cparams(("arbitrary",)),
        name="dispatch",
    )(pos_flat, h1, xs0)


def _expert_kernel(te_ref, nused_ref, xs_ref, wg_ref, wu_ref, wd_ref, bg_ref, bu_ref, bd_ref, ys_ref,
                   wgb, wub, wdb):
    i = pl.program_id(0)
    used = i < nused_ref[0]
    fresh = jnp.logical_or(i == 0, te_ref[i] != te_ref[jnp.maximum(i - 1, 0)])

    @pl.when(jnp.logical_and(used, fresh))
    def _():
        wgb[...] = wg_ref[0].astype(BF16)
        wub[...] = wu_ref[0].astype(BF16)
        wdb[...] = wd_ref[0].astype(BF16)

    @pl.when(used)
    def _():
        x = xs_ref[...].astype(BF16)
        g = jnp.minimum(jnp.dot(x, wgb[...], preferred_element_type=F32) + bg_ref[0], SWIGLU_LIMIT)
        u = jnp.clip(jnp.dot(x, wub[...], preferred_element_type=F32) + bu_ref[0], -SWIGLU_LIMIT, SWIGLU_LIMIT)
        act = (u + 1.0) * g * _sigmoid(SWIGLU_ALPHA * g)
        ys_ref[...] = jnp.dot(act.astype(BF16), wdb[...], preferred_element_type=F32) + bd_ref[0]

    @pl.when(jnp.logical_not(used))
    def _():
        ys_ref[...] = jnp.zeros_like(ys_ref)


def _experts(xs, tile_expert, n_used, w_gate, b_gate, w_up, b_up, w_down, b_down, tr):
    P = xs.shape[0]
    wmap = lambda i, te, nu: (te[i], 0, 0)
    return pl.pallas_call(
        _expert_kernel,
        grid_spec=pltpu.PrefetchScalarGridSpec(
            num_scalar_prefetch=2,
            grid=(P // tr,),
            in_specs=[pl.BlockSpec((tr, D_MODEL), lambda i, te, nu: (i, 0)),
                      pl.BlockSpec((1, D_MODEL, D_FF), wmap), pl.BlockSpec((1, D_MODEL, D_FF), wmap),
                      pl.BlockSpec((1, D_FF, D_MODEL), wmap),
                      pl.BlockSpec((1, 1, D_FF), wmap), pl.BlockSpec((1, 1, D_FF), wmap),
                      pl.BlockSpec((1, 1, D_MODEL), wmap)],
            out_specs=pl.BlockSpec((tr, D_MODEL), lambda i, te, nu: (i, 0)),
            scratch_shapes=[pltpu.VMEM((D_MODEL, D_FF), BF16), pltpu.VMEM((D_MODEL, D_FF), BF16),
                            pltpu.VMEM((D_FF, D_MODEL), BF16)]),
        out_shape=jax.ShapeDtypeStruct((P, D_MODEL), F32),
        compiler_params=_cparams(("arbitrary",)),
        name="experts",
    )(tile_expert, n_used, xs, w_gate, w_up, w_down, b_gate[:, None, :], b_up[:, None, :], b_down[:, None, :])


def _combine_kernel(pos_ref, h1_ref, gate_ref, g_ref, b_ref, ys_ref, out_ref, buf, sem, *, tm, alpha):
    base = pl.program_id(0) * (tm * TOP_K)

    def copy(t, r):
        src = pos_ref[base + t * TOP_K + r]
        return pltpu.make_async_copy(ys_ref.at[pl.ds(src, 1), :], buf.at[r, pl.ds(t, 1), :], sem)

    def start(t, _):
        for r in range(TOP_K):
            copy(t, r).start()
        return 0

    def wait(t, _):
        for r in range(TOP_K):
            copy(t, r).wait()
        return 0

    lax.fori_loop(0, tm, start, 0)
    lax.fori_loop(0, tm, wait, 0)
    gates = gate_ref[...]
    moe = gates[:, 0:1] * buf[0]
    for r in range(1, TOP_K):
        moe = moe + gates[:, r:r + 1] * buf[r]
    out_ref[...] = _layer_norm(alpha * h1_ref[...] + moe, g_ref[...], b_ref[...])


def _combine(h1, gates, ys, pos_flat, ln_g, ln_b, tm, alpha):
    N = h1.shape[0]
    kern = functools.partial(_combine_kernel, tm=tm, alpha=alpha)
    return pl.pallas_call(
        kern,
        grid_spec=pltpu.PrefetchScalarGridSpec(
            num_scalar_prefetch=1,
            grid=(N // tm,),
            in_specs=[pl.BlockSpec((tm, D_MODEL), lambda i, pos: (i, 0)),
                      pl.BlockSpec((tm, LANES), lambda i, pos: (i, 0)),
                      pl.BlockSpec((1, D_MODEL), lambda i, pos: (0, 0)),
                      pl.BlockSpec((1, D_MODEL), lambda i, pos: (0, 0)),
                      pl.BlockSpec(memory_space=pl.ANY)],
            out_specs=pl.BlockSpec((tm, D_MODEL), lambda i, pos: (i, 0)),
            scratch_shapes=[pltpu.VMEM((TOP_K, tm, D_MODEL), F32), pltpu.SemaphoreType.DMA(())]),
        out_shape=jax.ShapeDtypeStruct((N, D_MODEL), F32),
        compiler_params=_cparams(("arbitrary",)),
        name="combine",
    )(pos_flat, h1, gates, ln_g[None, :], ln_b[None, :], ys)


def _pick(n, prefs):
    for p in prefs:
        if n % p == 0:
            return p
    raise ValueError(f"no tile size in {prefs} divides {n}")


def _moe_layout(idx, rank, counts, tr):
    tiles = (counts + tr - 1) // tr
    tile_end = jnp.cumsum(tiles)
    offs = (tile_end - tiles) * tr
    hot = idx[:, :, None] == jnp.arange(N_EXPERTS, dtype=I32)[None, None, :]
    pos = jnp.sum(jnp.where(hot, offs[None, None, :], 0), axis=-1) + rank
    n_tiles = (idx.shape[0] * TOP_K) // tr + N_EXPERTS
    tile_id = jnp.arange(n_tiles, dtype=I32)
    tile_expert = jnp.sum((tile_end[None, :] <= tile_id[:, None]).astype(I32), axis=1)
    tile_expert = jnp.minimum(tile_expert, N_EXPERTS - 1).astype(I32)
    return pos.reshape(-1).astype(I32), tile_expert, tile_end[-1:].astype(I32), n_tiles * tr


def kernel(x, ln_in_g, ln_in_b, w_in, b_in, conv_w, conv_b, m_norm_g, w_out, ln1_g, ln1_b, w_router, b_router,
           w_gate, b_gate, w_up, b_up, w_down, b_down, ln2_g, ln2_b):
    B, T, D = x.shape
    depth = w_in.shape[0]
    assert D == D_MODEL and T % 256 == 0 and depth == 1
    alpha = (2.0 * depth) ** 0.25
    N = B * T
    tm_proj = _pick(T, (512, 256))
    L = 256
    TQ_SEL = 256
    TQ_ATT, TK_ATT = 256, _pick(T, (512, 256))
    tm_out = _pick(N, (512, 256))
    tm_tok = _pick(N, (128,))
    tr = 256

    h = x.reshape(N, D)
    for l in range(depth):
        wp, bp = _pack_in_weights(w_in[l], b_in[l])
        (h0, mq, mk, mv, mo_sig, aq, ak, av, xq, gm_sig, ga_sig, xk, small) = _in_proj(
            h, ln_in_g, ln_in_b, wp, bp, T, tm_proj)
        ym = _mlstm(mq, mk, mv, mo_sig, gm_sig, small, conv_w[l], conv_b[l], m_norm_g[l], B, T, L)
        mask = _select(xq, xk, small, B, T, TQ_SEL)
        ya = _attn(aq, ak, av, mask, ga_sig, B, T, TQ_ATT, TK_ATT)
        h1, gates, idx, rank, counts = _out_proj(ym, ya, h0, w_out[l].astype(BF16), ln1_g[l], ln1_b[l],
                                                 w_router[l], b_router[l], tm_out, alpha)
        pos, tile_expert, n_used, n_rows = _moe_layout(idx[:, :TOP_K], rank[:, :TOP_K],
                                                       counts[0, :N_EXPERTS].astype(I32), tr)
        xs = _dispatch(h1, pos, n_rows, tm_tok)
        ys = _experts(xs, tile_expert, n_used, w_gate[l], b_gate[l], w_up[l], b_up[l], w_down[l], b_down[l], tr)
        h = _combine(h1, gates, ys, pos, ln2_g[l], ln2_b[l], tm_tok, alpha)
    return h.reshape(B, T, D)
```

```python
import functools
import math

import jax
import jax.numpy as jnp
from jax import lax
from jax.experimental import pallas as pl
from jax.experimental.pallas import tpu as pltpu

F32 = jnp.float32
BF16 = jnp.bfloat16
I32 = jnp.int32

D_MODEL = 1024
CHUNK = 64
M_HEADS = 4
M_HEAD_DIM = D_MODEL // M_HEADS
M_WIDTH = M_HEADS * M_HEAD_DIM
CONV_WIDTH = 4
A_HEADS = 8
A_HEAD_DIM = D_MODEL // A_HEADS
A_WIDTH = A_HEADS * A_HEAD_DIM
IDX_HEADS = 8
IDX_DIM = 64
IDX_WIDTH = IDX_HEADS * IDX_DIM
TOPK_MAX = 256
ROPE_THETA = 10000.0
N_EXPERTS = 32
TOP_K = 4
D_FF = D_MODEL
SWIGLU_LIMIT = 7.0
SWIGLU_ALPHA = 1.702
LN_EPS = 1e-5

LANES = 128
NEG_BIG = -1e30
F32_TINY = 1.1754943508222875e-38
LOG2_E = 1.4426950408889634
VMEM_LIMIT = 56 * 1024 * 1024

_BIG_GROUPS = ("m_q", "m_k", "m_v", "m_o", "a_q", "a_k", "a_v", "x_q", "g_m", "g_a")
_GROUP_WIDTH = dict(m_q=M_WIDTH, m_k=M_WIDTH, m_v=M_WIDTH, m_o=M_WIDTH, a_q=A_WIDTH, a_k=A_WIDTH,
                    a_v=A_WIDTH, x_q=IDX_WIDTH, g_m=D_MODEL, g_a=D_MODEL, x_k=LANES, small=LANES)
_GROUP_ORDER = _BIG_GROUPS + ("x_k", "small")
_GROUP_START = {}
_off = 0
for _g in _GROUP_ORDER:
    _GROUP_START[_g] = _off
    _off += _GROUP_WIDTH[_g]
PACKED_WIDTH = _off


def _cparams(sem, vmem=VMEM_LIMIT):
    return pltpu.CompilerParams(dimension_semantics=sem, vmem_limit_bytes=vmem)


def _layer_norm(x, g, b):
    mu = jnp.mean(x, axis=-1, keepdims=True)
    xc = x - mu
    var = jnp.mean(xc * xc, axis=-1, keepdims=True)
    return xc * lax.rsqrt(var + LN_EPS) * g + b


def _sigmoid(x):
    return 1.0 / (1.0 + jnp.exp(-x))


def _in_proj_kernel(x_ref, g_ref, b_ref, w_ref, bias_ref, cosa_ref, sina_ref, cosb_ref, sinlo_ref, sinhi_ref,
                    h0_ref, mq_ref, mk_ref, mv_ref, mo_ref, aq_ref, ak_ref, av_ref, xq_ref, gm_ref, ga_ref,
                    xk_ref, small_ref):
    h0 = _layer_norm(x_ref[...], g_ref[...], b_ref[...])
    h0_ref[...] = h0
    hb = h0.astype(BF16)

    def proj(name, c0=0, width=None):
        start = _GROUP_START[name] + c0
        width = _GROUP_WIDTH[name] if width is None else width
        y = jnp.dot(hb, w_ref[:, start:start + width], preferred_element_type=F32)
        return y + bias_ref[:, start:start + width]

    def rope_full(y):
        return y * cosa_ref[...] + pltpu.roll(y, 64, axis=1) * sina_ref[...]

    def rope_half(y):
        return (y * cosb_ref[...] + pltpu.roll(y, 96, axis=1) * sinlo_ref[...]
                + pltpu.roll(y, 32, axis=1) * sinhi_ref[...])

    mq_ref[...] = proj("m_q").astype(BF16)
    mk_ref[...] = proj("m_k").astype(BF16)
    mv_ref[...] = proj("m_v").astype(BF16)
    mo_ref[...] = _sigmoid(proj("m_o")).astype(BF16)
    gm_ref[...] = _sigmoid(proj("g_m")).astype(BF16)
    ga_ref[...] = _sigmoid(proj("g_a")).astype(BF16)
    av_ref[...] = proj("a_v").astype(BF16)
    q_scale = A_HEAD_DIM ** -0.5 * LOG2_E
    for h in range(A_HEADS):
        sl = slice(h * LANES, (h + 1) * LANES)
        aq_ref[:, sl] = (rope_full(proj("a_q", h * LANES, LANES)) * q_scale).astype(BF16)
        ak_ref[:, sl] = rope_full(proj("a_k", h * LANES, LANES)).astype(BF16)
    for c in range(IDX_WIDTH // LANES):
        sl = slice(c * LANES, (c + 1) * LANES)
        xq_ref[:, sl] = rope_half(proj("x_q", c * LANES, LANES)).astype(BF16)
    xk_ref[...] = rope_half(proj("x_k")).astype(BF16)
    small_ref[...] = proj("small")


def _rope_tables(T):
    pos = jnp.arange(T, dtype=F32)[:, None]
    half_a = A_HEAD_DIM // 2
    inv_a = ROPE_THETA ** (-jnp.arange(half_a, dtype=F32) / half_a)
    ang_a = pos * inv_a[None, :]
    cos_a, sin_a = jnp.cos(ang_a), jnp.sin(ang_a)
    cosa = jnp.concatenate([cos_a, cos_a], axis=1)
    sina = jnp.concatenate([-sin_a, sin_a], axis=1)
    half_b = IDX_DIM // 2
    inv_b = ROPE_THETA ** (-jnp.arange(half_b, dtype=F32) / half_b)
    ang_b = pos * inv_b[None, :]
    cos_b, sin_b = jnp.cos(ang_b), jnp.sin(ang_b)
    zero = jnp.zeros_like(sin_b)
    cosb = jnp.concatenate([cos_b, cos_b, cos_b, cos_b], axis=1)
    sinlo = jnp.concatenate([-sin_b, zero, -sin_b, zero], axis=1)
    sinhi = jnp.concatenate([zero, sin_b, zero, sin_b], axis=1)
    return cosa, sina, cosb, sinlo, sinhi


def _pack_in_weights(w, b):
    sizes = (M_WIDTH, M_WIDTH, M_WIDTH, M_WIDTH, M_HEADS, M_HEADS, A_WIDTH, A_WIDTH, A_WIDTH,
             IDX_WIDTH, IDX_DIM, IDX_HEADS, D_MODEL, D_MODEL)
    names = ("m_q", "m_k", "m_v", "m_o", "m_i", "m_f", "a_q", "a_k", "a_v", "x_q", "x_k", "x_w", "g_m", "g_a")
    parts, start = {}, 0
    for n, s in zip(names, sizes):
        parts[n] = (w[:, start:start + s], b[start:start + s])
        start += s
    rows = w.shape[0]

    def pad(n_cols):
        return jnp.zeros((rows, n_cols), w.dtype), jnp.zeros((n_cols,), b.dtype)

    order = [parts[n] for n in _BIG_GROUPS]
    order += [parts["x_k"], pad(LANES - IDX_DIM)]
    order += [parts["m_i"], parts["m_f"], parts["x_w"], pad(LANES - 2 * M_HEADS - IDX_HEADS)]
    wp = jnp.concatenate([o[0] for o in order], axis=1).astype(BF16)
    bp = jnp.concatenate([o[1] for o in order], axis=0).astype(F32)[None, :]
    return wp, bp


def _in_proj(x2, ln_g, ln_b, wp, bp, T, tm):
    N = x2.shape[0]
    nt = T // tm
    tables = _rope_tables(T)
    row = lambda i: (i, 0)
    const = lambda i: (0, 0)
    tab = lambda i: (i % nt, 0)
    widths = [D_MODEL, M_WIDTH, M_WIDTH, M_WIDTH, M_WIDTH, A_WIDTH, A_WIDTH, A_WIDTH, IDX_WIDTH, D_MODEL, D_MODEL,
              LANES, LANES]
    dtypes = [F32] + [BF16] * 11 + [F32]
    return pl.pallas_call(
        _in_proj_kernel,
        grid=(N // tm,),
        in_specs=[pl.BlockSpec((tm, D_MODEL), row),
                  pl.BlockSpec((1, D_MODEL), const), pl.BlockSpec((1, D_MODEL), const),
                  pl.BlockSpec((D_MODEL, PACKED_WIDTH), const, pipeline_mode=pl.Buffered(1)),
                  pl.BlockSpec((1, PACKED_WIDTH), const)]
                 + [pl.BlockSpec((tm, LANES), tab)] * 5,
        out_specs=[pl.BlockSpec((tm, wd), row) for wd in widths],
        out_shape=[jax.ShapeDtypeStruct((N, wd), dt) for wd, dt in zip(widths, dtypes)],
        compiler_params=_cparams(("parallel",)),
        name="in_proj",
    )(x2, ln_g[None, :], ln_b[None, :], wp, bp, *tables)


def _log_sigmoid(x):
    return jnp.minimum(x, 0.0) - jnp.log(1.0 + jnp.exp(-jnp.abs(x)))


def _mlstm_kernel(q_ref, k_ref, v_ref, o_ref, gm_ref, gcol_ref, grow_ref, cwq_ref, cwk_ref, cbq_ref, cbk_ref,
                  ng_ref, out_ref, qext, kext, c_state, n_state, m_state, *, L):
    c = pl.program_id(2)
    halo = 8

    @pl.when(c == 0)
    def _():
        qext[0:halo, :] = jnp.zeros((halo, M_HEAD_DIM), F32)
        kext[0:halo, :] = jnp.zeros((halo, M_HEAD_DIM), F32)
        c_state[...] = jnp.zeros_like(c_state)
        n_state[...] = jnp.zeros_like(n_state)
        m_state[...] = jnp.zeros_like(m_state)

    def conv_silu(ext, x_ref, w_ref, b_ref):
        x = x_ref[...].astype(F32)
        ext[halo:halo + L, :] = x
        acc = b_ref[...] + ext[pl.ds(halo - 3, L), :] * w_ref[0:1, :]
        for j in range(1, CONV_WIDTH):
            acc = acc + ext[pl.ds(halo - 3 + j, L), :] * w_ref[j:j + 1, :]
        ext[0:halo, :] = x[L - halo:L, :]
        return acc * _sigmoid(acc)

    qc = conv_silu(qext, q_ref, cwq_ref, cbq_ref) * (M_HEAD_DIM ** -0.5)
    kc = conv_silu(kext, k_ref, cwk_ref, cbk_ref)
    v = v_ref[...]

    gcol = gcol_ref[0, 0]
    grow = grow_ref[0, 0]
    i_col, lf_col = gcol[:, 0:1], _log_sigmoid(gcol[:, 1:2])
    i_row, lf_row = grow[0:1, :], _log_sigmoid(grow[1:2, :])

    r = lax.broadcasted_iota(I32, (L, L), 0)
    s = lax.broadcasted_iota(I32, (L, L), 1)
    causal = r >= s
    hi = lax.Precision.HIGHEST
    b_col = jnp.dot(causal.astype(F32), jnp.broadcast_to(lf_col, (L, LANES)), precision=hi,
                    preferred_element_type=F32)[:, 0:1]
    b_row = jnp.dot(jnp.broadcast_to(lf_row, (8, L)), (r <= s).astype(F32), precision=hi,
                    preferred_element_type=F32)[0:1, :]

    m_prev = m_state[...]
    dm = jnp.where(causal, b_col - b_row + i_row, -jnp.inf)
    m_inter = b_col + m_prev
    m_t = jnp.maximum(m_inter, jnp.max(dm, axis=1, keepdims=True))
    qb, kb = qc.astype(BF16), kc.astype(BF16)
    qk = lax.dot_general(qb, kb, (((1,), (1,)), ((), ())), preferred_element_type=F32)
    sm = qk * jnp.exp(dm - m_t)
    w_inter = jnp.exp(m_inter - m_t)
    cb = c_state[...].astype(BF16)
    num = (jnp.dot(sm.astype(BF16), v, preferred_element_type=F32)
           + w_inter * jnp.dot(qb, cb, preferred_element_type=F32))
    den = (jnp.sum(sm, axis=1, keepdims=True)
           + w_inter * jnp.sum(qc * n_state[...], axis=1, keepdims=True))
    hh = num / jnp.maximum(jnp.abs(den), jnp.exp(-m_t))

    b_last = b_col[L - 1:L, :]
    g_row = b_last - b_row + i_row
    m_new = jnp.maximum(b_last + m_prev, jnp.max(g_row, axis=1, keepdims=True))
    wg_col = jnp.exp(b_last - b_col + i_col - m_new)
    decay = jnp.exp(b_last + m_prev - m_new)
    kw = kc * wg_col
    kv = lax.dot_general(kw.astype(BF16), v, (((0,), (0,)), ((), ())), preferred_element_type=F32)
    c_state[...] = decay * c_state[...] + kv
    n_state[...] = decay * n_state[...] + jnp.sum(kw, axis=0, keepdims=True)
    m_state[...] = m_new

    mu = jnp.mean(hh, axis=1, keepdims=True)
    hc = hh - mu
    var = jnp.mean(hc * hc, axis=1, keepdims=True)
    hn = hc * lax.rsqrt(var + LN_EPS) * ng_ref[...]
    out_ref[...] = (hn * o_ref[...].astype(F32) * gm_ref[...].astype(F32)).astype(BF16)


def _mlstm(mq, mk, mv, mo_sig, gm_sig, small, conv_w, conv_b, norm_g, B, T, L):
    nL = T // L
    i_pre = small[:, 0:M_HEADS].reshape(B, T, M_HEADS)
    f_pre = small[:, M_HEADS:2 * M_HEADS].reshape(B, T, M_HEADS)
    gates = jnp.stack([i_pre, f_pre], axis=-1)
    gcol = gates.transpose(0, 2, 1, 3)
    grow = gates.transpose(0, 2, 3, 1)
    blk = lambda b, h, c: (b * nL + c, h)
    head = lambda b, h, c: (0, h)
    kern = functools.partial(_mlstm_kernel, L=L)
    return pl.pallas_call(
        kern,
        grid=(B, M_HEADS, nL),
        in_specs=[pl.BlockSpec((L, M_HEAD_DIM), blk)] * 5
                 + [pl.BlockSpec((1, 1, L, 2), lambda b, h, c: (b, h, c, 0)),
                    pl.BlockSpec((1, 1, 2, L), lambda b, h, c: (b, h, 0, c)),
                    pl.BlockSpec((CONV_WIDTH, M_HEAD_DIM), head),
                    pl.BlockSpec((CONV_WIDTH, M_HEAD_DIM), head),
                    pl.BlockSpec((1, M_HEAD_DIM), head),
                    pl.BlockSpec((1, M_HEAD_DIM), head),
                    pl.BlockSpec((1, M_HEAD_DIM), head)],
        out_specs=pl.BlockSpec((L, M_HEAD_DIM), blk),
        out_shape=jax.ShapeDtypeStruct((B * T, M_WIDTH), BF16),
        scratch_shapes=[pltpu.VMEM((L + 8, M_HEAD_DIM), F32), pltpu.VMEM((L + 8, M_HEAD_DIM), F32),
                        pltpu.VMEM((M_HEAD_DIM, M_HEAD_DIM), F32), pltpu.VMEM((1, M_HEAD_DIM), F32),
                        pltpu.VMEM((1, 1), F32)],
        compiler_params=_cparams(("parallel", "parallel", "arbitrary")),
        name="mlstm",
    )(mq, mk, mv, mo_sig, gm_sig, gcol, grow,
      conv_w[:, :M_WIDTH], conv_w[:, M_WIDTH:], conv_b[None, :M_WIDTH], conv_b[None, M_WIDTH:], norm_g[None, :])


def _select_kernel(xq_ref, xk_ref, xw_ref, mask_ref, sc_ref, ti_ref, *, TQ, T, top_k):
    qi = pl.program_id(1)
    nkt = qi + 1
    SUB = 8
    NACC = 4
    kf = float(top_k)
    w_all = xw_ref[0] * ((IDX_DIM ** -0.5) * (IDX_HEADS ** -0.5))

    key_chunk = lax.broadcasted_iota(I32, (TQ, TQ), 0) // CHUNK
    qry_chunk = lax.broadcasted_iota(I32, (TQ, TQ), 1) // CHUNK
    diag_ok = key_chunk <= qry_chunk

    def fold(op, acc, tile):
        for i in range(TQ // SUB):
            acc = op(acc, tile[i * SUB:(i + 1) * SUB, :])
        return acc

    def score_tile(kt, carry):
        mx, mn = carry
        koff = pl.multiple_of(kt * TQ, TQ)
        kblk = xk_ref[pl.ds(koff, TQ), 0:IDX_DIM]
        acc = jnp.zeros((TQ, TQ), F32)
        for h in range(IDX_HEADS):
            qh = xq_ref[:, h * IDX_DIM:(h + 1) * IDX_DIM]
            sh = lax.dot_general(kblk, qh, (((1,), (1,)), ((), ())), preferred_element_type=F32)
            acc = acc + jnp.maximum(sh, 0.0) * w_all[h:h + 1, :]
        ok = jnp.logical_or(kt < qi, diag_ok)
        sc_ref[pl.ds(koff, TQ), :] = jnp.where(ok, acc, NEG_BIG)
        mx = fold(jnp.maximum, mx, jnp.where(ok, acc, NEG_BIG))
        mn = fold(jnp.minimum, mn, jnp.where(ok, acc, -NEG_BIG))
        return mx, mn

    mx, mn = lax.fori_loop(0, nkt, score_tile,
                           (jnp.full((SUB, TQ), NEG_BIG, F32), jnp.full((SUB, TQ), -NEG_BIG, F32)))
    rep = lambda v: jnp.broadcast_to(v, (SUB, TQ))
    mx = rep(jnp.max(mx, axis=0, keepdims=True))
    mn = rep(jnp.min(mn, axis=0, keepdims=True))

    def count(pred, ref=sc_ref):
        def body(kt, accs):
            koff = pl.multiple_of(kt * TQ, TQ)
            tile = ref[pl.ds(koff, TQ), :]
            accs = list(accs)
            for i in range(TQ // SUB):
                sv = tile[i * SUB:(i + 1) * SUB, :]
                accs[i % NACC] = accs[i % NACC] + jnp.where(pred(sv, koff + i * SUB), 1.0, 0.0)
            return tuple(accs)

        accs = lax.fori_loop(0, nkt, body, tuple(jnp.zeros((SUB, TQ), F32) for _ in range(NACC)))
        tot = accs[0]
        for a in accs[1:]:
            tot = tot + a
        return rep(jnp.sum(tot, axis=0, keepdims=True))

    def any_query(flag):
        return jnp.max(jnp.where(flag, 1.0, 0.0)) > 0.0

    q_id = qi * TQ + lax.broadcasted_iota(I32, (SUB, TQ), 1)
    n_adm = ((q_id // CHUNK + 1) * CHUNK).astype(F32)
    lo0 = mn
    hi0 = mx + jnp.maximum(jnp.abs(mx), 1e-30) * 1e-3
    c_zero = count(lambda sv, off: sv >= 0.0)
    c_pos = count(lambda sv, off: sv >= F32_TINY)
    pos_side = jnp.logical_and(c_pos >= kf, lo0 < F32_TINY)
    zero_hit = jnp.logical_and(c_pos < kf, c_zero >= kf)
    neg_side = jnp.logical_and(c_zero < kf, hi0 > 0.0)
    lo1 = jnp.where(pos_side, F32_TINY, jnp.where(zero_hit, 0.0, lo0))
    clo1 = jnp.where(pos_side, c_pos, jnp.where(zero_hit, c_zero, n_adm))
    hi1 = jnp.where(zero_hit, F32_TINY, jnp.where(neg_side, 0.0, hi0))

    def active(lo, hi, clo):
        mid = 0.5 * lo + 0.5 * hi
        return jnp.logical_and(clo > kf, jnp.logical_and(mid > lo, mid < hi)), mid

    def halve(lo, hi, clo):
        act, mid = active(lo, hi, clo)
        cnt = count(lambda sv, off: sv >= mid)
        ge = cnt >= kf
        up = jnp.logical_and(act, ge)
        lo = jnp.where(up, mid, lo)
        clo = jnp.where(up, cnt, clo)
        hi = jnp.where(jnp.logical_and(act, jnp.logical_not(ge)), mid, hi)
        return lo, hi, clo

    def cond(carry):
        return jnp.logical_and(carry[1], carry[0] < 200)

    def body(carry):
        it, _, lo, hi, clo = carry
        lo, hi, clo = halve(*halve(lo, hi, clo))
        return it + 1, any_query(active(lo, hi, clo)[0]), lo, hi, clo

    _, _, lo, hi, clo = lax.while_loop(
        cond, body, (jnp.int32(0), any_query(active(lo1, hi1, clo1)[0]), lo1, hi1, clo1))

    def key_index(off):
        return (off + lax.broadcasted_iota(I32, (SUB, TQ), 0)).astype(F32)

    def write_mask(keep):
        def write(kt, _):
            koff = pl.multiple_of(kt * TQ, TQ)
            tile = sc_ref[pl.ds(koff, TQ), :]
            slabs = []
            for i in range(TQ // SUB):
                sv = tile[i * SUB:(i + 1) * SUB, :]
                slabs.append(jnp.where(keep(sv, koff + i * SUB), 1.0, 0.0))
            mask_ref[:, pl.ds(koff, TQ)] = jnp.concatenate(slabs, axis=0).T.astype(jnp.int8)
            return 0

        lax.fori_loop(0, nkt, write, 0)

    tie = clo > kf
    has_tie = any_query(tie)

    @pl.when(jnp.logical_not(has_tie))
    def _():
        write_mask(lambda sv, off: sv >= lo)

    @pl.when(has_tie)
    def _():
        c_hi = count(lambda sv, off: sv >= hi)
        need = kf - c_hi
        n_tie = clo - c_hi

        def stage(kt, _):
            koff = pl.multiple_of(kt * TQ, TQ)
            tile = sc_ref[pl.ds(koff, TQ), :]
            slabs = []
            for i in range(TQ // SUB):
                sv = tile[i * SUB:(i + 1) * SUB, :]
                slabs.append(jnp.where(jnp.logical_and(sv >= lo, sv < hi), key_index(koff + i * SUB), 2.0 * T))
            ti_ref[pl.ds(koff, TQ), :] = jnp.concatenate(slabs, axis=0)
            return 0

        lax.fori_loop(0, nkt, stage, 0)

        def jactive(jlo, jhi):
            return jnp.logical_and(tie, jhi - jlo > 1.5)

        def jcond(carry):
            return jnp.logical_and(carry[1], carry[0] < 64)

        def jbody(carry):
            it, _, jlo, jhi, cjlo, cjhi = carry
            act = jactive(jlo, jhi)
            frac = (need - cjlo) / jnp.maximum(cjhi - cjlo, 1.0)
            jint = jnp.ceil(jlo + (jhi - jlo) * frac)
            jmid = jnp.floor(0.5 * (jlo + jhi))
            even = jnp.full((SUB, TQ), (it % 2 == 0).astype(F32), F32) > 0.5
            j = jnp.where(even, jnp.clip(jint, jlo + 1.0, jhi - 1.0), jmid)
            cnt = count(lambda tv, off: tv < j, ti_ref)
            ge = cnt >= need
            up = jnp.logical_and(act, ge)
            dn = jnp.logical_and(act, jnp.logical_not(ge))
            jhi = jnp.where(up, j, jhi)
            cjhi = jnp.where(up, cnt, cjhi)
            jlo = jnp.where(dn, j, jlo)
            cjlo = jnp.where(dn, cnt, cjlo)
            return it + 1, any_query(jactive(jlo, jhi)), jlo, jhi, cjlo, cjhi

        jlo0 = jnp.zeros((SUB, TQ), F32)
        jhi0 = jnp.full((SUB, TQ), float(T), F32)
        _, _, _, jhi, _, _ = lax.while_loop(
            jcond, jbody, (jnp.int32(0), any_query(jactive(jlo0, jhi0)), jlo0, jhi0, jlo0, n_tie))
        jsel = jnp.where(tie, jhi, float(T))
        write_mask(lambda sv, off: jnp.logical_and(
            sv >= lo, jnp.logical_or(sv >= hi, key_index(off) < jsel)))

    def zero_tile(kt, _):
        mask_ref[:, pl.ds(pl.multiple_of(kt * TQ, TQ), TQ)] = jnp.zeros((TQ, TQ), jnp.int8)
        return 0

    lax.fori_loop(nkt, T // TQ, zero_tile, 0)


def _select(xq, xk, small, B, T, TQ):
    nq = T // TQ
    top_k = min(TOPK_MAX, T // 4)
    xw = small[:, 2 * M_HEADS:2 * M_HEADS + IDX_HEADS].reshape(B, T, IDX_HEADS).transpose(0, 2, 1)
    kern = functools.partial(_select_kernel, TQ=TQ, T=T, top_k=top_k)
    return pl.pallas_call(
        kern,
        grid=(B, nq),
        in_specs=[pl.BlockSpec((TQ, IDX_WIDTH), lambda b, q: (b * nq + q, 0)),
                  pl.BlockSpec((T, LANES), lambda b, q: (b, 0)),
                  pl.BlockSpec((1, IDX_HEADS, TQ), lambda b, q: (b, 0, q))],
        out_specs=pl.BlockSpec((TQ, T), lambda b, q: (b * nq + q, 0)),
        out_shape=jax.ShapeDtypeStruct((B * T, T), jnp.int8),
        scratch_shapes=[pltpu.VMEM((T, TQ), F32), pltpu.VMEM((T, TQ), F32)],
        compiler_params=_cparams(("parallel", "parallel")),
        name="select",
    )(xq, xk, xw)


def _attn_kernel(q_ref, k_ref, v_ref, mask_ref, g_ref, out_ref, bias_ref, *head_refs, TQ, TK):
    vext_refs = head_refs[0:A_HEADS]
    m_refs = head_refs[A_HEADS:2 * A_HEADS]
    acc_refs = head_refs[2 * A_HEADS:3 * A_HEADS]
    qi = pl.program_id(1)
    kt = pl.program_id(2)
    nk = pl.num_programs(2)
    last = ((qi + 1) * TQ - 1) // TK

    @pl.when(kt == 0)
    def _():
        for h in range(A_HEADS):
            m_refs[h][...] = jnp.full((TQ, LANES), NEG_BIG, F32)
            acc_refs[h][...] = jnp.zeros((TQ, 2 * A_HEAD_DIM), F32)
            vext_refs[h][:, A_HEAD_DIM:] = jnp.ones((TK, A_HEAD_DIM), BF16)

    @pl.when(kt <= last)
    def _():
        bias_ref[...] = ((mask_ref[...].astype(F32) - 1.0) * (-NEG_BIG)).astype(BF16)
        for h in range(A_HEADS):
            sl = slice(h * A_HEAD_DIM, (h + 1) * A_HEAD_DIM)
            vext_refs[h][:, 0:A_HEAD_DIM] = v_ref[:, sl]
            s = lax.dot_general(q_ref[:, sl], k_ref[:, sl], (((1,), (1,)), ((), ())), preferred_element_type=F32)
            sb = s.astype(BF16) + bias_ref[...]
            m_old = m_refs[h][...]
            mx = jnp.max(sb, axis=1, keepdims=True).astype(F32)
            m_new = jnp.maximum(m_old, jnp.broadcast_to(mx, (TQ, LANES)))
            alpha = jnp.exp2(m_old - m_new)
            p = jnp.exp2(sb - m_new[:, 0:1].astype(BF16))
            pv = jnp.dot(p, vext_refs[h][...], preferred_element_type=F32)
            acc_refs[h][...] = jnp.concatenate([alpha, alpha], axis=1) * acc_refs[h][...] + pv
            m_refs[h][...] = m_new

    @pl.when(kt == nk - 1)
    def _():
        for h in range(A_HEADS):
            sl = slice(h * A_HEAD_DIM, (h + 1) * A_HEAD_DIM)
            num = acc_refs[h][:, 0:A_HEAD_DIM]
            den = acc_refs[h][:, A_HEAD_DIM:]
            out_ref[:, sl] = (num / den * g_ref[:, sl].astype(F32)).astype(BF16)


def _attn(aq, ak, av, mask, ga_sig, B, T, TQ, TK):
    nq, nk = T // TQ, T // TK

    def kv_map(b, q, k):
        return (b * nk + jnp.minimum(k, ((q + 1) * TQ - 1) // TK), 0)

    def mask_map(b, q, k):
        return (b * nq + q, jnp.minimum(k, ((q + 1) * TQ - 1) // TK))

    qmap = lambda b, q, k: (b * nq + q, 0)
    kern = functools.partial(_attn_kernel, TQ=TQ, TK=TK)
    return pl.pallas_call(
        kern,
        grid=(B, nq, nk),
        in_specs=[pl.BlockSpec((TQ, A_WIDTH), qmap),
                  pl.BlockSpec((TK, A_WIDTH), kv_map),
                  pl.BlockSpec((TK, A_WIDTH), kv_map),
                  pl.BlockSpec((TQ, TK), mask_map),
                  pl.BlockSpec((TQ, A_WIDTH), qmap)],
        out_specs=pl.BlockSpec((TQ, A_WIDTH), qmap),
        out_shape=jax.ShapeDtypeStruct((B * T, A_WIDTH), BF16),
        scratch_shapes=[pltpu.VMEM((TQ, TK), BF16)]
                       + [pltpu.VMEM((TK, 2 * A_HEAD_DIM), BF16)] * A_HEADS
                       + [pltpu.VMEM((TQ, LANES), F32)] * A_HEADS
                       + [pltpu.VMEM((TQ, 2 * A_HEAD_DIM), F32)] * A_HEADS,
        compiler_params=_cparams(("parallel", "parallel", "arbitrary")),
        name="attn",
    )(aq, ak, av, mask, ga_sig)


def _out_proj_kernel(ym_ref, ya_ref, h0_ref, wo_ref, g_ref, b_ref, wr_ref, br_ref,
                     h1_ref, gate_ref, idx_ref, rank_ref, cnt_ref, carry_ref, *, tm, alpha):
    i = pl.program_id(0)

    @pl.when(i == 0)
    def _():
        carry_ref[...] = jnp.zeros_like(carry_ref)

    merged = (ym_ref[...].astype(F32) + ya_ref[...].astype(F32)).astype(BF16)
    y = jnp.dot(merged, wo_ref[...], preferred_element_type=F32)
    h1 = _layer_norm(alpha * h0_ref[...] + y, g_ref[...], b_ref[...])
    h1_ref[...] = h1

    logits = jnp.dot(h1, wr_ref[...], precision=lax.Precision.HIGHEST, preferred_element_type=F32) + br_ref[...]
    lane = lax.broadcasted_iota(I32, (tm, LANES), 1)
    vals, hots = [], []
    idx_out = jnp.zeros((tm, LANES), I32)
    work = logits
    for r in range(TOP_K):
        mx = jnp.max(work, axis=1, keepdims=True)
        first = jnp.min(jnp.where(work == mx, lane, LANES), axis=1, keepdims=True)
        hot = lane == first
        vals.append(mx)
        hots.append(hot)
        idx_out = jnp.where(lane == r, first, idx_out)
        work = jnp.where(hot, -jnp.inf, work)
    exps = [jnp.exp(v - vals[0]) for v in vals]
    tot = exps[0] + exps[1] + exps[2] + exps[3]
    gate_out = jnp.zeros((tm, LANES), F32)
    chosen = jnp.zeros((tm, LANES), F32)
    for r in range(TOP_K):
        gate_out = jnp.where(lane == r, exps[r] / tot, gate_out)
        chosen = chosen + jnp.where(hots[r], 1.0, 0.0)

    rr = lax.broadcasted_iota(I32, (tm, tm), 0)
    cc = lax.broadcasted_iota(I32, (tm, tm), 1)
    strict = (rr > cc).astype(BF16)
    before = jnp.dot(strict, chosen.astype(BF16), preferred_element_type=F32) + carry_ref[...]
    rank_out = jnp.zeros((tm, LANES), I32)
    for r in range(TOP_K):
        rk = jnp.sum(jnp.where(hots[r], before, 0.0), axis=1, keepdims=True)
        rank_out = jnp.where(lane == r, rk.astype(I32), rank_out)
    carry_ref[...] = carry_ref[...] + jnp.sum(chosen, axis=0, keepdims=True)

    gate_ref[...] = gate_out
    idx_ref[...] = idx_out
    rank_ref[...] = rank_out
    cnt_ref[...] = carry_ref[...]


def _out_proj(ym, ya, h0, w_out_b, ln_g, ln_b, w_router, b_router, tm, alpha):
    N = ym.shape[0]
    wr = jnp.zeros((D_MODEL, LANES), F32).at[:, :N_EXPERTS].set(w_router)
    br = jnp.full((1, LANES), NEG_BIG, F32).at[0, :N_EXPERTS].set(b_router)
    row = lambda i: (i, 0)
    const = lambda i: (0, 0)
    kern = functools.partial(_out_proj_kernel, tm=tm, alpha=alpha)
    return pl.pallas_call(
        kern,
        grid=(N // tm,),
        in_specs=[pl.BlockSpec((tm, D_MODEL), row), pl.BlockSpec((tm, D_MODEL), row),
                  pl.BlockSpec((tm, D_MODEL), row),
                  pl.BlockSpec((D_MODEL, D_MODEL), const),
                  pl.BlockSpec((1, D_MODEL), const), pl.BlockSpec((1, D_MODEL), const),
                  pl.BlockSpec((D_MODEL, LANES), const), pl.BlockSpec((1, LANES), const)],
        out_specs=[pl.BlockSpec((tm, D_MODEL), row), pl.BlockSpec((tm, LANES), row),
                   pl.BlockSpec((tm, LANES), row), pl.BlockSpec((tm, LANES), row),
                   pl.BlockSpec((1, LANES), const)],
        out_shape=[jax.ShapeDtypeStruct((N, D_MODEL), F32), jax.ShapeDtypeStruct((N, LANES), F32),
                   jax.ShapeDtypeStruct((N, LANES), I32), jax.ShapeDtypeStruct((N, LANES), I32),
                   jax.ShapeDtypeStruct((1, LANES), F32)],
        scratch_shapes=[pltpu.VMEM((1, LANES), F32)],
        compiler_params=_cparams(("arbitrary",)),
        name="out_proj",
    )(ym, ya, h0, w_out_b, ln_g[None, :], ln_b[None, :], wr, br)


def _dispatch_kernel(pos_ref, h_ref, xs_in_ref, xs_ref, sem, *, tm):
    del xs_in_ref
    base = pl.program_id(0) * (tm * TOP_K)

    def copy(t, r):
        dst = pos_ref[base + t * TOP_K + r]
        return pltpu.make_async_copy(h_ref.at[pl.ds(t, 1), :], xs_ref.at[pl.ds(dst, 1), :], sem)

    def start(t, _):
        for r in range(TOP_K):
            copy(t, r).start()
        return 0

    def wait(t, _):
        for r in range(TOP_K):
            copy(t, r).wait()
        return 0

    lax.fori_loop(0, tm, start, 0)
    lax.fori_loop(0, tm, wait, 0)


def _dispatch(h1, pos_flat, n_rows, tm):
    N = h1.shape[0]
    xs0 = jnp.zeros((n_rows, D_MODEL), F32)
    kern = functools.partial(_dispatch_kernel, tm=tm)
    return pl.pallas_call(
        kern,
        grid_spec=pltpu.PrefetchScalarGridSpec(
            num_scalar_prefetch=1,
            grid=(N // tm,),
            in_specs=[pl.BlockSpec((tm, D_MODEL), lambda i, pos: (i, 0)),
                      pl.BlockSpec(memory_space=pl.ANY)],
            out_specs=pl.BlockSpec(memory_space=pl.ANY),
            scratch_shapes=[pltpu.SemaphoreType.DMA(())]),
        out_shape=jax.ShapeDtypeStruct((n_rows, D_MODEL), F32),
        input_output_aliases={2: 0},
        compiler_params=_cparams(("arbitrary",)),
        name="dispatch",
    )(pos_flat, h1, xs0)


def _expert_kernel(te_ref, nused_ref, xs_ref, wg_ref, wu_ref, wd_ref, bg_ref, bu_ref, bd_ref, ys_ref,
                   wgb, wub, wdb):
    i = pl.program_id(0)
    used = i < nused_ref[0]
    fresh = jnp.logical_or(i == 0, te_ref[i] != te_ref[jnp.maximum(i - 1, 0)])

    @pl.when(jnp.logical_and(used, fresh))
    def _():
        wgb[...] = wg_ref[0].astype(BF16)
        wub[...] = wu_ref[0].astype(BF16)
        wdb[...] = wd_ref[0].astype(BF16)

    @pl.when(used)
    def _():
        x = xs_ref[...].astype(BF16)
        g = jnp.minimum(jnp.dot(x, wgb[...], preferred_element_type=F32) + bg_ref[0], SWIGLU_LIMIT)
        u = jnp.clip(jnp.dot(x, wub[...], preferred_element_type=F32) + bu_ref[0], -SWIGLU_LIMIT, SWIGLU_LIMIT)
        act = (u + 1.0) * g * _sigmoid(SWIGLU_ALPHA * g)
        ys_ref[...] = jnp.dot(act.astype(BF16), wdb[...], preferred_element_type=F32) + bd_ref[0]

    @pl.when(jnp.logical_not(used))
    def _():
        ys_ref[...] = jnp.zeros_like(ys_ref)


def _experts(xs, tile_expert, n_used, w_gate, b_gate, w_up, b_up, w_down, b_down, tr):
    P = xs.shape[0]
    wmap = lambda i, te, nu: (te[i], 0, 0)
    return pl.pallas_call(
        _expert_kernel,
        grid_spec=pltpu.PrefetchScalarGridSpec(
            num_scalar_prefetch=2,
            grid=(P // tr,),
            in_specs=[pl.BlockSpec((tr, D_MODEL), lambda i, te, nu: (i, 0)),
                      pl.BlockSpec((1, D_MODEL, D_FF), wmap), pl.BlockSpec((1, D_MODEL, D_FF), wmap),
                      pl.BlockSpec((1, D_FF, D_MODEL), wmap),
                      pl.BlockSpec((1, 1, D_FF), wmap), pl.BlockSpec((1, 1, D_FF), wmap),
                      pl.BlockSpec((1, 1, D_MODEL), wmap)],
            out_specs=pl.BlockSpec((tr, D_MODEL), lambda i, te, nu: (i, 0)),
            scratch_shapes=[pltpu.VMEM((D_MODEL, D_FF), BF16), pltpu.VMEM((D_MODEL, D_FF), BF16),
                            pltpu.VMEM((D_FF, D_MODEL), BF16)]),
        out_shape=jax.ShapeDtypeStruct((P, D_MODEL), F32),
        compiler_params=_cparams(("arbitrary",)),
        name="experts",
    )(tile_expert, n_used, xs, w_gate, w_up, w_down, b_gate[:, None, :], b_up[:, None, :], b_down[:, None, :])


def _combine_kernel(pos_ref, h1_ref, gate_ref, g_ref, b_ref, ys_ref, out_ref, buf, sem, *, tm, alpha):
    base = pl.program_id(0) * (tm * TOP_K)

    def copy(t, r):
        src = pos_ref[base + t * TOP_K + r]
        return pltpu.make_async_copy(ys_ref.at[pl.ds(src, 1), :], buf.at[r, pl.ds(t, 1), :], sem)

    def start(t, _):
        for r in range(TOP_K):
            copy(t, r).start()
        return 0

    def wait(t, _):
        for r in range(TOP_K):
            copy(t, r).wait()
        return 0

    lax.fori_loop(0, tm, start, 0)
    lax.fori_loop(0, tm, wait, 0)
    gates = gate_ref[...]
    moe = gates[:, 0:1] * buf[0]
    for r in range(1, TOP_K):
        moe = moe + gates[:, r:r + 1] * buf[r]
    out_ref[...] = _layer_norm(alpha * h1_ref[...] + moe, g_ref[...], b_ref[...])


def _combine(h1, gates, ys, pos_flat, ln_g, ln_b, tm, alpha):
    N = h1.shape[0]
    kern = functools.partial(_combine_kernel, tm=tm, alpha=alpha)
    return pl.pallas_call(
        kern,
        grid_spec=pltpu.PrefetchScalarGridSpec(
            num_scalar_prefetch=1,
            grid=(N // tm,),
            in_specs=[pl.BlockSpec((tm, D_MODEL), lambda i, pos: (i, 0)),
                      pl.BlockSpec((tm, LANES), lambda i, pos: (i, 0)),
                      pl.BlockSpec((1, D_MODEL), lambda i, pos: (0, 0)),
                      pl.BlockSpec((1, D_MODEL), lambda i, pos: (0, 0)),
                      pl.BlockSpec(memory_space=pl.ANY)],
            out_specs=pl.BlockSpec((tm, D_MODEL), lambda i, pos: (i, 0)),
            scratch_shapes=[pltpu.VMEM((TOP_K, tm, D_MODEL), F32), pltpu.SemaphoreType.DMA(())]),
        out_shape=jax.ShapeDtypeStruct((N, D_MODEL), F32),
        compiler_params=_cparams(("arbitrary",)),
        name="combine",
    )(pos_flat, h1, gates, ln_g[None, :], ln_b[None, :], ys)


def _pick(n, prefs):
    for p in prefs:
        if n % p == 0:
            return p
    raise ValueError(f"no tile size in {prefs} divides {n}")


def _moe_layout(idx, rank, counts, tr):
    tiles = (counts + tr - 1) // tr
    tile_end = jnp.cumsum(tiles)
    offs = (tile_end - tiles) * tr
    hot = idx[:, :, None] == jnp.arange(N_EXPERTS, dtype=I32)[None, None, :]
    pos = jnp.sum(jnp.where(hot, offs[None, None, :], 0), axis=-1) + rank
    n_tiles = (idx.shape[0] * TOP_K) // tr + N_EXPERTS
    tile_id = jnp.arange(n_tiles, dtype=I32)
    tile_expert = jnp.sum((tile_end[None, :] <= tile_id[:, None]).astype(I32), axis=1)
    tile_expert = jnp.minimum(tile_expert, N_EXPERTS - 1).astype(I32)
    return pos.reshape(-1).astype(I32), tile_expert, tile_end[-1:].astype(I32), n_tiles * tr


def kernel(x, ln_in_g, ln_in_b, w_in, b_in, conv_w, conv_b, m_norm_g, w_out, ln1_g, ln1_b, w_router, b_router,
           w_gate, b_gate, w_up, b_up, w_down, b_down, ln2_g, ln2_b):
    B, T, D = x.shape
    depth = w_in.shape[0]
    assert D == D_MODEL and T % 256 == 0 and depth == 1
    alpha = (2.0 * depth) ** 0.25
    N = B * T
    tm_proj = _pick(T, (512, 256))
    L = 256
    TQ_SEL = 256
    TQ_ATT, TK_ATT = 512, _pick(T, (512, 256))
    tm_out = _pick(N, (512, 256))
    tm_tok = _pick(N, (128,))
    tr = 256

    h = x.reshape(N, D)
    for l in range(depth):
        wp, bp = _pack_in_weights(w_in[l], b_in[l])
        (h0, mq, mk, mv, mo_sig, aq, ak, av, xq, gm_sig, ga_sig, xk, small) = _in_proj(
            h, ln_in_g, ln_in_b, wp, bp, T, tm_proj)
        ym = _mlstm(mq, mk, mv, mo_sig, gm_sig, small, conv_w[l], conv_b[l], m_norm_g[l], B, T, L)
        mask = _select(xq, xk, small, B, T, TQ_SEL)
        ya = _attn(aq, ak, av, mask, ga_sig, B, T, TQ_ATT, TK_ATT)
        h1, gates, idx, rank, counts = _out_proj(ym, ya, h0, w_out[l].astype(BF16), ln1_g[l], ln1_b[l],
                                                 w_router[l], b_router[l], tm_out, alpha)
        pos, tile_expert, n_used, n_rows = _moe_layout(idx[:, :TOP_K], rank[:, :TOP_K],
                                                       counts[0, :N_EXPERTS].astype(I32), tr)
        xs = _dispatch(h1, pos, n_rows, tm_tok)
        ys = _experts(xs, tile_expert, n_used, w_gate[l], b_gate[l], w_up[l], b_up[l], w_down[l], b_down[l], tr)
        h = _combine(h1, gates, ys, pos, ln2_g[l], ln2_b[l], tm_tok, alpha)
    return h.reshape(B, T, D)
```

```python
import functools
import math

import jax
import jax.numpy as jnp
from jax import lax
from jax.experimental import pallas as pl
from jax.experimental.pallas import tpu as pltpu

F32 = jnp.float32
BF16 = jnp.bfloat16
I32 = jnp.int32

D_MODEL = 1024
CHUNK = 64
M_HEADS = 4
M_HEAD_DIM = D_MODEL // M_HEADS
M_WIDTH = M_HEADS * M_HEAD_DIM
CONV_WIDTH = 4
A_HEADS = 8
A_HEAD_DIM = D_MODEL // A_HEADS
A_WIDTH = A_HEADS * A_HEAD_DIM
IDX_HEADS = 8
IDX_DIM = 64
IDX_WIDTH = IDX_HEADS * IDX_DIM
TOPK_MAX = 256
ROPE_THETA = 10000.0
N_EXPERTS = 32
TOP_K = 4
D_FF = D_MODEL
SWIGLU_LIMIT = 7.0
SWIGLU_ALPHA = 1.702
LN_EPS = 1e-5

LANES = 128
NEG_BIG = -1e30
F32_TINY = 1.1754943508222875e-38
LOG2_E = 1.4426950408889634
VMEM_LIMIT = 56 * 1024 * 1024

_BIG_GROUPS = ("m_q", "m_k", "m_v", "m_o", "a_q", "a_k", "a_v", "x_q", "g_m", "g_a")
_GROUP_WIDTH = dict(m_q=M_WIDTH, m_k=M_WIDTH, m_v=M_WIDTH, m_o=M_WIDTH, a_q=A_WIDTH, a_k=A_WIDTH,
                    a_v=A_WIDTH, x_q=IDX_WIDTH, g_m=D_MODEL, g_a=D_MODEL, x_k=LANES, small=LANES)
_GROUP_ORDER = _BIG_GROUPS + ("x_k", "small")
_GROUP_START = {}
_off = 0
for _g in _GROUP_ORDER:
    _GROUP_START[_g] = _off
    _off += _GROUP_WIDTH[_g]
PACKED_WIDTH = _off


def _cparams(sem, vmem=VMEM_LIMIT):
    return pltpu.CompilerParams(dimension_semantics=sem, vmem_limit_bytes=vmem)


def _layer_norm(x, g, b):
    mu = jnp.mean(x, axis=-1, keepdims=True)
    xc = x - mu
    var = jnp.mean(xc * xc, axis=-1, keepdims=True)
    return xc * lax.rsqrt(var + LN_EPS) * g + b


def _sigmoid(x):
    return 1.0 / (1.0 + jnp.exp(-x))


def _in_proj_kernel(x_ref, g_ref, b_ref, w_ref, bias_ref, cosa_ref, sina_ref, cosb_ref, sinlo_ref, sinhi_ref,
                    h0_ref, mq_ref, mk_ref, mv_ref, mo_ref, aq_ref, ak_ref, av_ref, xq_ref, gm_ref, ga_ref,
                    xk_ref, small_ref):
    h0 = _layer_norm(x_ref[...], g_ref[...], b_ref[...])
    h0_ref[...] = h0
    hb = h0.astype(BF16)

    def proj(name, c0=0, width=None):
        start = _GROUP_START[name] + c0
        width = _GROUP_WIDTH[name] if width is None else width
        y = jnp.dot(hb, w_ref[:, start:start + width], preferred_element_type=F32)
        return y + bias_ref[:, start:start + width]

    def rope_full(y):
        return y * cosa_ref[...] + pltpu.roll(y, 64, axis=1) * sina_ref[...]

    def rope_half(y):
        return (y * cosb_ref[...] + pltpu.roll(y, 96, axis=1) * sinlo_ref[...]
                + pltpu.roll(y, 32, axis=1) * sinhi_ref[...])

    mq_ref[...] = proj("m_q").astype(BF16)
    mk_ref[...] = proj("m_k").astype(BF16)
    mv_ref[...] = proj("m_v").astype(BF16)
    mo_ref[...] = _sigmoid(proj("m_o")).astype(BF16)
    gm_ref[...] = _sigmoid(proj("g_m")).astype(BF16)
    ga_ref[...] = _sigmoid(proj("g_a")).astype(BF16)
    av_ref[...] = proj("a_v").astype(BF16)
    q_scale = A_HEAD_DIM ** -0.5 * LOG2_E
    for h in range(A_HEADS):
        sl = slice(h * LANES, (h + 1) * LANES)
        aq_ref[:, sl] = (rope_full(proj("a_q", h * LANES, LANES)) * q_scale).astype(BF16)
        ak_ref[:, sl] = rope_full(proj("a_k", h * LANES, LANES)).astype(BF16)
    for c in range(IDX_WIDTH // LANES):
        sl = slice(c * LANES, (c + 1) * LANES)
        xq_ref[:, sl] = rope_half(proj("x_q", c * LANES, LANES)).astype(BF16)
    xk_ref[...] = rope_half(proj("x_k")).astype(BF16)
    small_ref[...] = proj("small")


def _rope_tables(T):
    pos = jnp.arange(T, dtype=F32)[:, None]
    half_a = A_HEAD_DIM // 2
    inv_a = ROPE_THETA ** (-jnp.arange(half_a, dtype=F32) / half_a)
    ang_a = pos * inv_a[None, :]
    cos_a, sin_a = jnp.cos(ang_a), jnp.sin(ang_a)
    cosa = jnp.concatenate([cos_a, cos_a], axis=1)
    sina = jnp.concatenate([-sin_a, sin_a], axis=1)
    half_b = IDX_DIM // 2
    inv_b = ROPE_THETA ** (-jnp.arange(half_b, dtype=F32) / half_b)
    ang_b = pos * inv_b[None, :]
    cos_b, sin_b = jnp.cos(ang_b), jnp.sin(ang_b)
    zero = jnp.zeros_like(sin_b)
    cosb = jnp.concatenate([cos_b, cos_b, cos_b, cos_b], axis=1)
    sinlo = jnp.concatenate([-sin_b, zero, -sin_b, zero], axis=1)
    sinhi = jnp.concatenate([zero, sin_b, zero, sin_b], axis=1)
    return cosa, sina, cosb, sinlo, sinhi


def _pack_in_weights(w, b):
    sizes = (M_WIDTH, M_WIDTH, M_WIDTH, M_WIDTH, M_HEADS, M_HEADS, A_WIDTH, A_WIDTH, A_WIDTH,
             IDX_WIDTH, IDX_DIM, IDX_HEADS, D_MODEL, D_MODEL)
    names = ("m_q", "m_k", "m_v", "m_o", "m_i", "m_f", "a_q", "a_k", "a_v", "x_q", "x_k", "x_w", "g_m", "g_a")
    parts, start = {}, 0
    for n, s in zip(names, sizes):
        parts[n] = (w[:, start:start + s], b[start:start + s])
        start += s
    rows = w.shape[0]

    def pad(n_cols):
        return jnp.zeros((rows, n_cols), w.dtype), jnp.zeros((n_cols,), b.dtype)

    order = [parts[n] for n in _BIG_GROUPS]
    order += [parts["x_k"], pad(LANES - IDX_DIM)]
    order += [parts["m_i"], parts["m_f"], parts["x_w"], pad(LANES - 2 * M_HEADS - IDX_HEADS)]
    wp = jnp.concatenate([o[0] for o in order], axis=1).astype(BF16)
    bp = jnp.concatenate([o[1] for o in order], axis=0).astype(F32)[None, :]
    return wp, bp


def _in_proj(x2, ln_g, ln_b, wp, bp, T, tm):
    N = x2.shape[0]
    nt = T // tm
    tables = _rope_tables(T)
    row = lambda i: (i, 0)
    const = lambda i: (0, 0)
    tab = lambda i: (i % nt, 0)
    widths = [D_MODEL, M_WIDTH, M_WIDTH, M_WIDTH, M_WIDTH, A_WIDTH, A_WIDTH, A_WIDTH, IDX_WIDTH, D_MODEL, D_MODEL,
              LANES, LANES]
    dtypes = [F32] + [BF16] * 11 + [F32]
    return pl.pallas_call(
        _in_proj_kernel,
        grid=(N // tm,),
        in_specs=[pl.BlockSpec((tm, D_MODEL), row),
                  pl.BlockSpec((1, D_MODEL), const), pl.BlockSpec((1, D_MODEL), const),
                  pl.BlockSpec((D_MODEL, PACKED_WIDTH), const, pipeline_mode=pl.Buffered(1)),
                  pl.BlockSpec((1, PACKED_WIDTH), const)]
                 + [pl.BlockSpec((tm, LANES), tab)] * 5,
        out_specs=[pl.BlockSpec((tm, wd), row) for wd in widths],
        out_shape=[jax.ShapeDtypeStruct((N, wd), dt) for wd, dt in zip(widths, dtypes)],
        compiler_params=_cparams(("parallel",)),
        name="in_proj",
    )(x2, ln_g[None, :], ln_b[None, :], wp, bp, *tables)


def _log_sigmoid(x):
    return jnp.minimum(x, 0.0) - jnp.log(1.0 + jnp.exp(-jnp.abs(x)))


def _mlstm_kernel(q_ref, k_ref, v_ref, o_ref, gm_ref, gcol_ref, grow_ref, cwq_ref, cwk_ref, cbq_ref, cbk_ref,
                  ng_ref, out_ref, qext, kext, c_state, n_state, m_state, *, L):
    c = pl.program_id(2)
    halo = 8

    @pl.when(c == 0)
    def _():
        qext[0:halo, :] = jnp.zeros((halo, M_HEAD_DIM), F32)
        kext[0:halo, :] = jnp.zeros((halo, M_HEAD_DIM), F32)
        c_state[...] = jnp.zeros_like(c_state)
        n_state[...] = jnp.zeros_like(n_state)
        m_state[...] = jnp.zeros_like(m_state)

    def conv_silu(ext, x_ref, w_ref, b_ref):
        x = x_ref[...].astype(F32)
        ext[halo:halo + L, :] = x
        acc = b_ref[...] + ext[pl.ds(halo - 3, L), :] * w_ref[0:1, :]
        for j in range(1, CONV_WIDTH):
            acc = acc + ext[pl.ds(halo - 3 + j, L), :] * w_ref[j:j + 1, :]
        ext[0:halo, :] = x[L - halo:L, :]
        return acc * _sigmoid(acc)

    qc = conv_silu(qext, q_ref, cwq_ref, cbq_ref) * (M_HEAD_DIM ** -0.5)
    kc = conv_silu(kext, k_ref, cwk_ref, cbk_ref)
    v = v_ref[...]

    gcol = gcol_ref[0, 0]
    grow = grow_ref[0, 0]
    i_col, lf_col = gcol[:, 0:1], _log_sigmoid(gcol[:, 1:2])
    i_row, lf_row = grow[0:1, :], _log_sigmoid(grow[1:2, :])

    r = lax.broadcasted_iota(I32, (L, L), 0)
    s = lax.broadcasted_iota(I32, (L, L), 1)
    causal = r >= s
    hi = lax.Precision.HIGHEST
    b_col = jnp.dot(causal.astype(F32), jnp.broadcast_to(lf_col, (L, LANES)), precision=hi,
                    preferred_element_type=F32)[:, 0:1]
    b_row = jnp.dot(jnp.broadcast_to(lf_row, (8, L)), (r <= s).astype(F32), precision=hi,
                    preferred_element_type=F32)[0:1, :]

    m_prev = m_state[...]
    dm = jnp.where(causal, b_col - b_row + i_row, -jnp.inf)
    m_inter = b_col + m_prev
    m_t = jnp.maximum(m_inter, jnp.max(dm, axis=1, keepdims=True))
    qb, kb = qc.astype(BF16), kc.astype(BF16)
    qk = lax.dot_general(qb, kb, (((1,), (1,)), ((), ())), preferred_element_type=F32)
    sm = qk * jnp.exp(dm - m_t)
    w_inter = jnp.exp(m_inter - m_t)
    cb = c_state[...].astype(BF16)
    num = (jnp.dot(sm.astype(BF16), v, preferred_element_type=F32)
           + w_inter * jnp.dot(qb, cb, preferred_element_type=F32))
    den = (jnp.sum(sm, axis=1, keepdims=True)
           + w_inter * jnp.sum(qc * n_state[...], axis=1, keepdims=True))
    hh = num / jnp.maximum(jnp.abs(den), jnp.exp(-m_t))

    b_last = b_col[L - 1:L, :]
    g_row = b_last - b_row + i_row
    m_new = jnp.maximum(b_last + m_prev, jnp.max(g_row, axis=1, keepdims=True))
    wg_col = jnp.exp(b_last - b_col + i_col - m_new)
    decay = jnp.exp(b_last + m_prev - m_new)
    kw = kc * wg_col
    kv = lax.dot_general(kw.astype(BF16), v, (((0,), (0,)), ((), ())), preferred_element_type=F32)
    c_state[...] = decay * c_state[...] + kv
    n_state[...] = decay * n_state[...] + jnp.sum(kw, axis=0, keepdims=True)
    m_state[...] = m_new

    mu = jnp.mean(hh, axis=1, keepdims=True)
    hc = hh - mu
    var = jnp.mean(hc * hc, axis=1, keepdims=True)
    hn = hc * lax.rsqrt(var + LN_EPS) * ng_ref[...]
    out_ref[...] = (hn * o_ref[...].astype(F32) * gm_ref[...].astype(F32)).astype(BF16)


def _mlstm(mq, mk, mv, mo_sig, gm_sig, small, conv_w, conv_b, norm_g, B, T, L):
    nL = T // L
    i_pre = small[:, 0:M_HEADS].reshape(B, T, M_HEADS)
    f_pre = small[:, M_HEADS:2 * M_HEADS].reshape(B, T, M_HEADS)
    gates = jnp.stack([i_pre, f_pre], axis=-1)
    gcol = gates.transpose(0, 2, 1, 3)
    grow = gates.transpose(0, 2, 3, 1)
    blk = lambda b, h, c: (b * nL + c, h)
    head = lambda b, h, c: (0, h)
    kern = functools.partial(_mlstm_kernel, L=L)
    return pl.pallas_call(
        kern,
        grid=(B, M_HEADS, nL),
        in_specs=[pl.BlockSpec((L, M_HEAD_DIM), blk)] * 5
                 + [pl.BlockSpec((1, 1, L, 2), lambda b, h, c: (b, h, c, 0)),
                    pl.BlockSpec((1, 1, 2, L), lambda b, h, c: (b, h, 0, c)),
                    pl.BlockSpec((CONV_WIDTH, M_HEAD_DIM), head),
                    pl.BlockSpec((CONV_WIDTH, M_HEAD_DIM), head),
                    pl.BlockSpec((1, M_HEAD_DIM), head),
                    pl.BlockSpec((1, M_HEAD_DIM), head),
                    pl.BlockSpec((1, M_HEAD_DIM), head)],
        out_specs=pl.BlockSpec((L, M_HEAD_DIM), blk),
        out_shape=jax.ShapeDtypeStruct((B * T, M_WIDTH), BF16),
        scratch_shapes=[pltpu.VMEM((L + 8, M_HEAD_DIM), F32), pltpu.VMEM((L + 8, M_HEAD_DIM), F32),
                        pltpu.VMEM((M_HEAD_DIM, M_HEAD_DIM), F32), pltpu.VMEM((1, M_HEAD_DIM), F32),
                        pltpu.VMEM((1, 1), F32)],
        compiler_params=_cparams(("parallel", "parallel", "arbitrary")),
        name="mlstm",
    )(mq, mk, mv, mo_sig, gm_sig, gcol, grow,
      conv_w[:, :M_WIDTH], conv_w[:, M_WIDTH:], conv_b[None, :M_WIDTH], conv_b[None, M_WIDTH:], norm_g[None, :])


def _select_kernel(xq_ref, xk_ref, xw_ref, mask_ref, sc_ref, ti_ref, *raw_refs, TQ, T, top_k):
    qi = pl.program_id(1)
    nkt = qi + 1
    SUB = 8
    NACC = 4
    kf = float(top_k)
    w_all = xw_ref[0] * ((IDX_DIM ** -0.5) * (IDX_HEADS ** -0.5))

    key_chunk = lax.broadcasted_iota(I32, (TQ, TQ), 0) // CHUNK
    qry_chunk = lax.broadcasted_iota(I32, (TQ, TQ), 1) // CHUNK
    diag_ok = key_chunk <= qry_chunk

    def fold(op, acc, tile):
        for i in range(TQ // SUB):
            acc = op(acc, tile[i * SUB:(i + 1) * SUB, :])
        return acc

    def products(kt, raw_ref):
        koff = pl.multiple_of(jnp.minimum(kt, nkt - 1) * TQ, TQ)
        kblk = xk_ref[pl.ds(koff, TQ), 0:IDX_DIM]
        for h in range(IDX_HEADS):
            qh = xq_ref[:, h * IDX_DIM:(h + 1) * IDX_DIM]
            raw_ref[h] = lax.dot_general(kblk, qh, (((1,), (1,)), ((), ())), preferred_element_type=F32)

    def combine(kt, raw_ref, mx, mn):
        kt = jnp.minimum(kt, nkt - 1)
        koff = pl.multiple_of(kt * TQ, TQ)
        acc = jnp.zeros((TQ, TQ), F32)
        for h in range(IDX_HEADS):
            acc = acc + jnp.maximum(raw_ref[h], 0.0) * w_all[h:h + 1, :]
        ok = jnp.logical_or(kt < qi, diag_ok)
        sc_ref[pl.ds(koff, TQ), :] = jnp.where(ok, acc, NEG_BIG)
        mx = fold(jnp.maximum, mx, jnp.where(ok, acc, NEG_BIG))
        mn = fold(jnp.minimum, mn, jnp.where(ok, acc, -NEG_BIG))
        return mx, mn

    def score_pair(j, carry):
        mx, mn = carry
        products(2 * j + 1, raw_refs[1])
        mx, mn = combine(2 * j, raw_refs[0], mx, mn)
        products(2 * j + 2, raw_refs[0])
        return combine(2 * j + 1, raw_refs[1], mx, mn)

    products(0, raw_refs[0])
    mx, mn = lax.fori_loop(0, (nkt + 1) // 2, score_pair,
                           (jnp.full((SUB, TQ), NEG_BIG, F32), jnp.full((SUB, TQ), -NEG_BIG, F32)))
    rep = lambda v: jnp.broadcast_to(v, (SUB, TQ))
    mx = rep(jnp.max(mx, axis=0, keepdims=True))
    mn = rep(jnp.min(mn, axis=0, keepdims=True))

    def count(pred, ref=sc_ref):
        def body(kt, accs):
            koff = pl.multiple_of(kt * TQ, TQ)
            tile = ref[pl.ds(koff, TQ), :]
            accs = list(accs)
            for i in range(TQ // SUB):
                sv = tile[i * SUB:(i + 1) * SUB, :]
                accs[i % NACC] = accs[i % NACC] + jnp.where(pred(sv, koff + i * SUB), 1.0, 0.0)
            return tuple(accs)

        accs = lax.fori_loop(0, nkt, body, tuple(jnp.zeros((SUB, TQ), F32) for _ in range(NACC)))
        tot = accs[0]
        for a in accs[1:]:
            tot = tot + a
        return rep(jnp.sum(tot, axis=0, keepdims=True))

    def any_query(flag):
        return jnp.max(jnp.where(flag, 1.0, 0.0)) > 0.0

    q_id = qi * TQ + lax.broadcasted_iota(I32, (SUB, TQ), 1)
    n_adm = ((q_id // CHUNK + 1) * CHUNK).astype(F32)
    lo0 = mn
    hi0 = mx + jnp.maximum(jnp.abs(mx), 1e-30) * 1e-3
    c_zero = count(lambda sv, off: sv >= 0.0)
    c_pos = count(lambda sv, off: sv >= F32_TINY)
    pos_side = jnp.logical_and(c_pos >= kf, lo0 < F32_TINY)
    zero_hit = jnp.logical_and(c_pos < kf, c_zero >= kf)
    neg_side = jnp.logical_and(c_zero < kf, hi0 > 0.0)
    lo1 = jnp.where(pos_side, F32_TINY, jnp.where(zero_hit, 0.0, lo0))
    clo1 = jnp.where(pos_side, c_pos, jnp.where(zero_hit, c_zero, n_adm))
    hi1 = jnp.where(zero_hit, F32_TINY, jnp.where(neg_side, 0.0, hi0))

    def active(lo, hi, clo):
        mid = 0.5 * lo + 0.5 * hi
        return jnp.logical_and(clo > kf, jnp.logical_and(mid > lo, mid < hi)), mid

    def halve(lo, hi, clo):
        act, mid = active(lo, hi, clo)
        cnt = count(lambda sv, off: sv >= mid)
        ge = cnt >= kf
        up = jnp.logical_and(act, ge)
        lo = jnp.where(up, mid, lo)
        clo = jnp.where(up, cnt, clo)
        hi = jnp.where(jnp.logical_and(act, jnp.logical_not(ge)), mid, hi)
        return lo, hi, clo

    def cond(carry):
        return jnp.logical_and(carry[1], carry[0] < 200)

    def body(carry):
        it, _, lo, hi, clo = carry
        lo, hi, clo = halve(*halve(lo, hi, clo))
        return it + 1, any_query(active(lo, hi, clo)[0]), lo, hi, clo

    _, _, lo, hi, clo = lax.while_loop(
        cond, body, (jnp.int32(0), any_query(active(lo1, hi1, clo1)[0]), lo1, hi1, clo1))

    def key_index(off):
        return (off + lax.broadcasted_iota(I32, (SUB, TQ), 0)).astype(F32)

    def write_mask(keep):
        def write(kt, _):
            koff = pl.multiple_of(kt * TQ, TQ)
            tile = sc_ref[pl.ds(koff, TQ), :]
            slabs = []
            for i in range(TQ // SUB):
                sv = tile[i * SUB:(i + 1) * SUB, :]
                slabs.append(jnp.where(keep(sv, koff + i * SUB), 1, 0))
            mask_ref[:, pl.ds(koff, TQ)] = jnp.concatenate(slabs, axis=0).T.astype(jnp.int8)
            return 0

        lax.fori_loop(0, nkt, write, 0)

    tie = clo > kf
    has_tie = any_query(tie)

    @pl.when(jnp.logical_not(has_tie))
    def _():
        write_mask(lambda sv, off: sv >= lo)

    @pl.when(has_tie)
    def _():
        c_hi = count(lambda sv, off: sv >= hi)
        need = kf - c_hi
        n_tie = clo - c_hi

        def stage(kt, _):
            koff = pl.multiple_of(kt * TQ, TQ)
            tile = sc_ref[pl.ds(koff, TQ), :]
            slabs = []
            for i in range(TQ // SUB):
                sv = tile[i * SUB:(i + 1) * SUB, :]
                slabs.append(jnp.where(jnp.logical_and(sv >= lo, sv < hi), key_index(koff + i * SUB), 2.0 * T))
            ti_ref[pl.ds(koff, TQ), :] = jnp.concatenate(slabs, axis=0)
            return 0

        lax.fori_loop(0, nkt, stage, 0)

        def jactive(jlo, jhi):
            return jnp.logical_and(tie, jhi - jlo > 1.5)

        def jcond(carry):
            return jnp.logical_and(carry[1], carry[0] < 64)

        def jbody(carry):
            it, _, jlo, jhi, cjlo, cjhi = carry
            act = jactive(jlo, jhi)
            frac = (need - cjlo) / jnp.maximum(cjhi - cjlo, 1.0)
            jint = jnp.ceil(jlo + (jhi - jlo) * frac)
            jmid = jnp.floor(0.5 * (jlo + jhi))
            even = jnp.full((SUB, TQ), (it % 2 == 0).astype(F32), F32) > 0.5
            j = jnp.where(even, jnp.clip(jint, jlo + 1.0, jhi - 1.0), jmid)
            cnt = count(lambda tv, off: tv < j, ti_ref)
            ge = cnt >= need
            up = jnp.logical_and(act, ge)
            dn = jnp.logical_and(act, jnp.logical_not(ge))
            jhi = jnp.where(up, j, jhi)
            cjhi = jnp.where(up, cnt, cjhi)
            jlo = jnp.where(dn, j, jlo)
            cjlo = jnp.where(dn, cnt, cjlo)
            return it + 1, any_query(jactive(jlo, jhi)), jlo, jhi, cjlo, cjhi

        jlo0 = jnp.zeros((SUB, TQ), F32)
        jhi0 = jnp.full((SUB, TQ), float(T), F32)
        _, _, _, jhi, _, _ = lax.while_loop(
            jcond, jbody, (jnp.int32(0), any_query(jactive(jlo0, jhi0)), jlo0, jhi0, jlo0, n_tie))
        jsel = jnp.where(tie, jhi, float(T))
        write_mask(lambda sv, off: jnp.logical_and(
            sv >= lo, jnp.logical_or(sv >= hi, key_index(off) < jsel)))

    def zero_tile(kt, _):
        mask_ref[:, pl.ds(pl.multiple_of(kt * TQ, TQ), TQ)] = jnp.zeros((TQ, TQ), jnp.int8)
        return 0

    lax.fori_loop(nkt, T // TQ, zero_tile, 0)


def _select(xq, xk, small, B, T, TQ):
    nq = T // TQ
    top_k = min(TOPK_MAX, T // 4)
    xw = small[:, 2 * M_HEADS:2 * M_HEADS + IDX_HEADS].reshape(B, T, IDX_HEADS).transpose(0, 2, 1)
    kern = functools.partial(_select_kernel, TQ=TQ, T=T, top_k=top_k)
    return pl.pallas_call(
        kern,
        grid=(B, nq),
        in_specs=[pl.BlockSpec((TQ, IDX_WIDTH), lambda b, q: (b * nq + q, 0)),
                  pl.BlockSpec((T, LANES), lambda b, q: (b, 0)),
                  pl.BlockSpec((1, IDX_HEADS, TQ), lambda b, q: (b, 0, q))],
        out_specs=pl.BlockSpec((TQ, T), lambda b, q: (b * nq + q, 0)),
        out_shape=jax.ShapeDtypeStruct((B * T, T), jnp.int8),
        scratch_shapes=[pltpu.VMEM((T, TQ), F32), pltpu.VMEM((T, TQ), F32),
                        pltpu.VMEM((IDX_HEADS, TQ, TQ), F32), pltpu.VMEM((IDX_HEADS, TQ, TQ), F32)],
        compiler_params=_cparams(("parallel", "parallel")),
        name="select",
    )(xq, xk, xw)


def _attn_kernel(q_ref, k_ref, v_ref, mask_ref, g_ref, out_ref, bias_ref, *head_refs, TQ, TK):
    vext_refs = head_refs[0:A_HEADS]
    m_refs = head_refs[A_HEADS:2 * A_HEADS]
    acc_refs = head_refs[2 * A_HEADS:3 * A_HEADS]
    qi = pl.program_id(1)
    kt = pl.program_id(2)
    nk = pl.num_programs(2)
    last = ((qi + 1) * TQ - 1) // TK

    @pl.when(kt == 0)
    def _():
        for h in range(A_HEADS):
            m_refs[h][...] = jnp.full((TQ, LANES), NEG_BIG, F32)
            acc_refs[h][...] = jnp.zeros((TQ, 2 * A_HEAD_DIM), F32)
            vext_refs[h][:, A_HEAD_DIM:] = jnp.ones((TK, A_HEAD_DIM), BF16)

    @pl.when(kt <= last)
    def _():
        bias_ref[...] = ((mask_ref[...].astype(F32) - 1.0) * (-NEG_BIG)).astype(BF16)
        for h in range(A_HEADS):
            sl = slice(h * A_HEAD_DIM, (h + 1) * A_HEAD_DIM)
            vext_refs[h][:, 0:A_HEAD_DIM] = v_ref[:, sl]
            s = lax.dot_general(q_ref[:, sl], k_ref[:, sl], (((1,), (1,)), ((), ())), preferred_element_type=F32)
            sb = s.astype(BF16) + bias_ref[...]
            m_old = m_refs[h][...]
            mx = jnp.max(sb, axis=1, keepdims=True).astype(F32)
            m_new = jnp.maximum(m_old, jnp.broadcast_to(mx, (TQ, LANES)))
            alpha = jnp.exp2(m_old - m_new)
            p = jnp.exp2(sb - m_new[:, 0:1].astype(BF16))
            pv = jnp.dot(p, vext_refs[h][...], preferred_element_type=F32)
            acc_refs[h][...] = jnp.concatenate([alpha, alpha], axis=1) * acc_refs[h][...] + pv
            m_refs[h][...] = m_new

    @pl.when(kt == nk - 1)
    def _():
        for h in range(A_HEADS):
            sl = slice(h * A_HEAD_DIM, (h + 1) * A_HEAD_DIM)
            num = acc_refs[h][:, 0:A_HEAD_DIM]
            den = acc_refs[h][:, A_HEAD_DIM:]
            out_ref[:, sl] = (num / den * g_ref[:, sl].astype(F32)).astype(BF16)


def _attn(aq, ak, av, mask, ga_sig, B, T, TQ, TK):
    nq, nk = T // TQ, T // TK

    def kv_map(b, q, k):
        return (b * nk + jnp.minimum(k, ((q + 1) * TQ - 1) // TK), 0)

    def mask_map(b, q, k):
        return (b * nq + q, jnp.minimum(k, ((q + 1) * TQ - 1) // TK))

    qmap = lambda b, q, k: (b * nq + q, 0)
    kern = functools.partial(_attn_kernel, TQ=TQ, TK=TK)
    return pl.pallas_call(
        kern,
        grid=(B, nq, nk),
        in_specs=[pl.BlockSpec((TQ, A_WIDTH), qmap),
                  pl.BlockSpec((TK, A_WIDTH), kv_map),
                  pl.BlockSpec((TK, A_WIDTH), kv_map),
                  pl.BlockSpec((TQ, TK), mask_map),
                  pl.BlockSpec((TQ, A_WIDTH), qmap)],
        out_specs=pl.BlockSpec((TQ, A_WIDTH), qmap),
        out_shape=jax.ShapeDtypeStruct((B * T, A_WIDTH), BF16),
        scratch_shapes=[pltpu.VMEM((TQ, TK), BF16)]
                       + [pltpu.VMEM((TK, 2 * A_HEAD_DIM), BF16)] * A_HEADS
                       + [pltpu.VMEM((TQ, LANES), F32)] * A_HEADS
                       + [pltpu.VMEM((TQ, 2 * A_HEAD_DIM), F32)] * A_HEADS,
        compiler_params=_cparams(("parallel", "parallel", "arbitrary")),
        name="attn",
    )(aq, ak, av, mask, ga_sig)


def _out_proj_kernel(ym_ref, ya_ref, h0_ref, wo_ref, g_ref, b_ref, wr_ref, br_ref,
                     h1_ref, gate_ref, idx_ref, rank_ref, cnt_ref, carry_ref, *, tm, alpha):
    i = pl.program_id(0)

    @pl.when(i == 0)
    def _():
        carry_ref[...] = jnp.zeros_like(carry_ref)

    merged = (ym_ref[...].astype(F32) + ya_ref[...].astype(F32)).astype(BF16)
    y = jnp.dot(merged, wo_ref[...], preferred_element_type=F32)
    h1 = _layer_norm(alpha * h0_ref[...] + y, g_ref[...], b_ref[...])
    h1_ref[...] = h1

    logits = jnp.dot(h1, wr_ref[...], precision=lax.Precision.HIGHEST, preferred_element_type=F32) + br_ref[...]
    lane = lax.broadcasted_iota(I32, (tm, LANES), 1)
    vals, hots = [], []
    idx_out = jnp.zeros((tm, LANES), I32)
    work = logits
    for r in range(TOP_K):
        mx = jnp.max(work, axis=1, keepdims=True)
        first = jnp.min(jnp.where(work == mx, lane, LANES), axis=1, keepdims=True)
        hot = lane == first
        vals.append(mx)
        hots.append(hot)
        idx_out = jnp.where(lane == r, first, idx_out)
        work = jnp.where(hot, -jnp.inf, work)
    exps = [jnp.exp(v - vals[0]) for v in vals]
    tot = exps[0] + exps[1] + exps[2] + exps[3]
    gate_out = jnp.zeros((tm, LANES), F32)
    chosen = jnp.zeros((tm, LANES), F32)
    for r in range(TOP_K):
        gate_out = jnp.where(lane == r, exps[r] / tot, gate_out)
        chosen = chosen + jnp.where(hots[r], 1.0, 0.0)

    rr = lax.broadcasted_iota(I32, (tm, tm), 0)
    cc = lax.broadcasted_iota(I32, (tm, tm), 1)
    strict = (rr > cc).astype(BF16)
    before = jnp.dot(strict, chosen.astype(BF16), preferred_element_type=F32) + carry_ref[...]
    rank_out = jnp.zeros((tm, LANES), I32)
    for r in range(TOP_K):
        rk = jnp.sum(jnp.where(hots[r], before, 0.0), axis=1, keepdims=True)
        rank_out = jnp.where(lane == r, rk.astype(I32), rank_out)
    carry_ref[...] = carry_ref[...] + jnp.sum(chosen, axis=0, keepdims=True)

    gate_ref[...] = gate_out
    idx_ref[...] = idx_out
    rank_ref[...] = rank_out
    cnt_ref[...] = carry_ref[...]


def _out_proj(ym, ya, h0, w_out_b, ln_g, ln_b, w_router, b_router, tm, alpha):
    N = ym.shape[0]
    wr = jnp.zeros((D_MODEL, LANES), F32).at[:, :N_EXPERTS].set(w_router)
    br = jnp.full((1, LANES), NEG_BIG, F32).at[0, :N_EXPERTS].set(b_router)
    row = lambda i: (i, 0)
    const = lambda i: (0, 0)
    kern = functools.partial(_out_proj_kernel, tm=tm, alpha=alpha)
    return pl.pallas_call(
        kern,
        grid=(N // tm,),
        in_specs=[pl.BlockSpec((tm, D_MODEL), row), pl.BlockSpec((tm, D_MODEL), row),
                  pl.BlockSpec((tm, D_MODEL), row),
                  pl.BlockSpec((D_MODEL, D_MODEL), const),
                  pl.BlockSpec((1, D_MODEL), const), pl.BlockSpec((1, D_MODEL), const),
                  pl.BlockSpec((D_MODEL, LANES), const), pl.BlockSpec((1, LANES), const)],
        out_specs=[pl.BlockSpec((tm, D_MODEL), row), pl.BlockSpec((tm, LANES), row),
                   pl.BlockSpec((tm, LANES), row), pl.BlockSpec((tm, LANES), row),
                   pl.BlockSpec((1, LANES), const)],
        out_shape=[jax.ShapeDtypeStruct((N, D_MODEL), F32), jax.ShapeDtypeStruct((N, LANES), F32),
                   jax.ShapeDtypeStruct((N, LANES), I32), jax.ShapeDtypeStruct((N, LANES), I32),
                   jax.ShapeDtypeStruct((1, LANES), F32)],
        scratch_shapes=[pltpu.VMEM((1, LANES), F32)],
        compiler_params=_cparams(("arbitrary",)),
        name="out_proj",
    )(ym, ya, h0, w_out_b, ln_g[None, :], ln_b[None, :], wr, br)


def _dispatch_kernel(pos_ref, tail_ref, nused_ref, h_ref, xs_ref, zero_ref, sem, *, tm, tr, n_tiles):
    base = pl.program_id(0) * (tm * TOP_K)

    @pl.when(pl.program_id(0) == 0)
    def _():
        zero_ref[...] = jnp.zeros_like(zero_ref)

        def clear(row0):
            return pltpu.make_async_copy(zero_ref, xs_ref.at[pl.ds(pl.multiple_of(row0, tr), tr), :], sem)

        def start_tail(e, _):
            @pl.when(tail_ref[e] >= 0)
            def _():
                clear(jnp.maximum(tail_ref[e], 0)).start()
            return 0

        def wait_tail(e, _):
            @pl.when(tail_ref[e] >= 0)
            def _():
                clear(jnp.maximum(tail_ref[e], 0)).wait()
            return 0

        def start_unused(t, _):
            clear(t * tr).start()
            return 0

        def wait_unused(t, _):
            clear(t * tr).wait()
            return 0

        lax.fori_loop(0, N_EXPERTS, start_tail, 0)
        lax.fori_loop(nused_ref[0], n_tiles, start_unused, 0)
        lax.fori_loop(0, N_EXPERTS, wait_tail, 0)
        lax.fori_loop(nused_ref[0], n_tiles, wait_unused, 0)

    def copy(t, r):
        dst = pos_ref[base + t * TOP_K + r]
        return pltpu.make_async_copy(h_ref.at[pl.ds(t, 1), :], xs_ref.at[pl.ds(dst, 1), :], sem)

    def start(t, _):
        for r in range(TOP_K):
            copy(t, r).start()
        return 0

    def wait(t, _):
        for r in range(TOP_K):
            copy(t, r).wait()
        return 0

    lax.fori_loop(0, tm, start, 0)
    lax.fori_loop(0, tm, wait, 0)


def _dispatch(h1, pos_flat, tail_row, n_used, n_rows, tm, tr):
    N = h1.shape[0]
    kern = functools.partial(_dispatch_kernel, tm=tm, tr=tr, n_tiles=n_rows // tr)
    return pl.pallas_call(
        kern,
        grid_spec=pltpu.PrefetchScalarGridSpec(
            num_scalar_prefetch=3,
            grid=(N // tm,),
            in_specs=[pl.BlockSpec((tm, D_MODEL), lambda i, pos, tail, nused: (i, 0))],
            out_specs=pl.BlockSpec(memory_space=pl.ANY),
            scratch_shapes=[pltpu.VMEM((tr, D_MODEL), F32), pltpu.SemaphoreType.DMA(())]),
        out_shape=jax.ShapeDtypeStruct((n_rows, D_MODEL), F32),
        compiler_params=_cparams(("arbitrary",)),
        name="dispatch",
    )(pos_flat, tail_row, n_used, h1)


def _expert_kernel(te_ref, nused_ref, xs_ref, wg_ref, wu_ref, wd_ref, bg_ref, bu_ref, bd_ref, ys_ref,
                   wgb, wub, wdb):
    i = pl.program_id(0)
    used = i < nused_ref[0]
    fresh = jnp.logical_or(i == 0, te_ref[i] != te_ref[jnp.maximum(i - 1, 0)])

    @pl.when(jnp.logical_and(used, fresh))
    def _():
        wgb[...] = wg_ref[0].astype(BF16)
        wub[...] = wu_ref[0].astype(BF16)
        wdb[...] = wd_ref[0].astype(BF16)

    @pl.when(used)
    def _():
        x = xs_ref[...].astype(BF16)
        g = jnp.minimum(jnp.dot(x, wgb[...], preferred_element_type=F32) + bg_ref[0], SWIGLU_LIMIT)
        u = jnp.clip(jnp.dot(x, wub[...], preferred_element_type=F32) + bu_ref[0], -SWIGLU_LIMIT, SWIGLU_LIMIT)
        act = (u + 1.0) * g * _sigmoid(SWIGLU_ALPHA * g)
        ys_ref[...] = jnp.dot(act.astype(BF16), wdb[...], preferred_element_type=F32) + bd_ref[0]

    @pl.when(jnp.logical_not(used))
    def _():
        ys_ref[...] = jnp.zeros_like(ys_ref)


def _experts(xs, tile_expert, n_used, w_gate, b_gate, w_up, b_up, w_down, b_down, tr):
    P = xs.shape[0]
    wmap = lambda i, te, nu: (te[i], 0, 0)
    return pl.pallas_call(
        _expert_kernel,
        grid_spec=pltpu.PrefetchScalarGridSpec(
            num_scalar_prefetch=2,
            grid=(P // tr,),
            in_specs=[pl.BlockSpec((tr, D_MODEL), lambda i, te, nu: (i, 0)),
                      pl.BlockSpec((1, D_MODEL, D_FF), wmap), pl.BlockSpec((1, D_MODEL, D_FF), wmap),
                      pl.BlockSpec((1, D_FF, D_MODEL), wmap),
                      pl.BlockSpec((1, 1, D_FF), wmap), pl.BlockSpec((1, 1, D_FF), wmap),
                      pl.BlockSpec((1, 1, D_MODEL), wmap)],
            out_specs=pl.BlockSpec((tr, D_MODEL), lambda i, te, nu: (i, 0)),
            scratch_shapes=[pltpu.VMEM((D_MODEL, D_FF), BF16), pltpu.VMEM((D_MODEL, D_FF), BF16),
                            pltpu.VMEM((D_FF, D_MODEL), BF16)]),
        out_shape=jax.ShapeDtypeStruct((P, D_MODEL), F32),
        compiler_params=_cparams(("arbitrary",)),
        name="experts",
    )(tile_expert, n_used, xs, w_gate, w_up, w_down, b_gate[:, None, :], b_up[:, None, :], b_down[:, None, :])


def _combine_kernel(pos_ref, h1_ref, gate_ref, g_ref, b_ref, ys_ref, out_ref, buf, sem, *, tm, alpha):
    base = pl.program_id(0) * (tm * TOP_K)

    def copy(t, r):
        src = pos_ref[base + t * TOP_K + r]
        return pltpu.make_async_copy(ys_ref.at[pl.ds(src, 1), :], buf.at[r, pl.ds(t, 1), :], sem)

    def start(t, _):
        for r in range(TOP_K):
            copy(t, r).start()
        return 0

    def wait(t, _):
        for r in range(TOP_K):
            copy(t, r).wait()
        return 0

    lax.fori_loop(0, tm, start, 0)
    lax.fori_loop(0, tm, wait, 0)
    gates = gate_ref[...]
    moe = gates[:, 0:1] * buf[0]
    for r in range(1, TOP_K):
        moe = moe + gates[:, r:r + 1] * buf[r]
    out_ref[...] = _layer_norm(alpha * h1_ref[...] + moe, g_ref[...], b_ref[...])


def _combine(h1, gates, ys, pos_flat, ln_g, ln_b, tm, alpha):
    N = h1.shape[0]
    kern = functools.partial(_combine_kernel, tm=tm, alpha=alpha)
    return pl.pallas_call(
        kern,
        grid_spec=pltpu.PrefetchScalarGridSpec(
            num_scalar_prefetch=1,
            grid=(N // tm,),
            in_specs=[pl.BlockSpec((tm, D_MODEL), lambda i, pos: (i, 0)),
                      pl.BlockSpec((tm, LANES), lambda i, pos: (i, 0)),
                      pl.BlockSpec((1, D_MODEL), lambda i, pos: (0, 0)),
                      pl.BlockSpec((1, D_MODEL), lambda i, pos: (0, 0)),
                      pl.BlockSpec(memory_space=pl.ANY)],
            out_specs=pl.BlockSpec((tm, D_MODEL), lambda i, pos: (i, 0)),
            scratch_shapes=[pltpu.VMEM((TOP_K, tm, D_MODEL), F32), pltpu.SemaphoreType.DMA(())]),
        out_shape=jax.ShapeDtypeStruct((N, D_MODEL), F32),
        compiler_params=_cparams(("arbitrary",)),
        name="combine",
    )(pos_flat, h1, gates, ln_g[None, :], ln_b[None, :], ys)


def _pick(n, prefs):
    for p in prefs:
        if n % p == 0:
            return p
    raise ValueError(f"no tile size in {prefs} divides {n}")


def _moe_layout(idx, rank, counts, tr):
    tiles = (counts + tr - 1) // tr
    tile_end = jnp.cumsum(tiles)
    offs = (tile_end - tiles) * tr
    hot = idx[:, :, None] == jnp.arange(N_EXPERTS, dtype=I32)[None, None, :]
    pos = jnp.sum(jnp.where(hot, offs[None, None, :], 0), axis=-1) + rank
    n_tiles = (idx.shape[0] * TOP_K) // tr + N_EXPERTS
    tile_id = jnp.arange(n_tiles, dtype=I32)
    tile_expert = jnp.sum((tile_end[None, :] <= tile_id[:, None]).astype(I32), axis=1)
    tile_expert = jnp.minimum(tile_expert, N_EXPERTS - 1).astype(I32)
    tail_row = jnp.where(tiles > 0, (tile_end - 1) * tr, -1).astype(I32)
    return pos.reshape(-1).astype(I32), tile_expert, tile_end[-1:].astype(I32), tail_row, n_tiles * tr


def kernel(x, ln_in_g, ln_in_b, w_in, b_in, conv_w, conv_b, m_norm_g, w_out, ln1_g, ln1_b, w_router, b_router,
           w_gate, b_gate, w_up, b_up, w_down, b_down, ln2_g, ln2_b):
    B, T, D = x.shape
    depth = w_in.shape[0]
    assert D == D_MODEL and T % 256 == 0 and depth == 1
    alpha = (2.0 * depth) ** 0.25
    N = B * T
    tm_proj = _pick(T, (512, 256))
    L = 256
    TQ_SEL = 256
    TQ_ATT, TK_ATT = 512, _pick(T, (512, 256))
    tm_out = _pick(N, (512, 256))
    tm_disp = _pick(N, (512, 256))
    tm_comb = _pick(N, (256,))
    tr = 256

    h = x.reshape(N, D)
    for l in range(depth):
        wp, bp = _pack_in_weights(w_in[l], b_in[l])
        (h0, mq, mk, mv, mo_sig, aq, ak, av, xq, gm_sig, ga_sig, xk, small) = _in_proj(
            h, ln_in_g, ln_in_b, wp, bp, T, tm_proj)
        ym = _mlstm(mq, mk, mv, mo_sig, gm_sig, small, conv_w[l], conv_b[l], m_norm_g[l], B, T, L)
        mask = _select(xq, xk, small, B, T, TQ_SEL)
        ya = _attn(aq, ak, av, mask, ga_sig, B, T, TQ_ATT, TK_ATT)
        h1, gates, idx, rank, counts = _out_proj(ym, ya, h0, w_out[l].astype(BF16), ln1_g[l], ln1_b[l],
                                                 w_router[l], b_router[l], tm_out, alpha)
        pos, tile_expert, n_used, tail_row, n_rows = _moe_layout(idx[:, :TOP_K], rank[:, :TOP_K],
                                                                 counts[0, :N_EXPERTS].astype(I32), tr)
        xs = _dispatch(h1, pos, tail_row, n_used, n_rows, tm_disp, tr)
        ys = _experts(xs, tile_expert, n_used, w_gate[l], b_gate[l], w_up[l], b_up[l], w_down[l], b_down[l], tr)
        h = _combine(h1, gates, ys, pos, ln2_g[l], ln2_b[l], tm_comb, alpha)
    return h.reshape(B, T, D)
```

```python
import functools
import math

import jax
import jax.numpy as jnp
from jax import lax
from jax.experimental import pallas as pl
from jax.experimental.pallas import tpu as pltpu

F32 = jnp.float32
BF16 = jnp.bfloat16
I32 = jnp.int32

D_MODEL = 1024
CHUNK = 64
M_HEADS = 4
M_HEAD_DIM = D_MODEL // M_HEADS
M_WIDTH = M_HEADS * M_HEAD_DIM
CONV_WIDTH = 4
A_HEADS = 8
A_HEAD_DIM = D_MODEL // A_HEADS
A_WIDTH = A_HEADS * A_HEAD_DIM
IDX_HEADS = 8
IDX_DIM = 64
IDX_WIDTH = IDX_HEADS * IDX_DIM
TOPK_MAX = 256
ROPE_THETA = 10000.0
N_EXPERTS = 32
TOP_K = 4
D_FF = D_MODEL
SWIGLU_LIMIT = 7.0
SWIGLU_ALPHA = 1.702
LN_EPS = 1e-5

LANES = 128
NEG_BIG = -1e30
F32_TINY = 1.1754943508222875e-38
LOG2_E = 1.4426950408889634
VMEM_LIMIT = 56 * 1024 * 1024

_BIG_GROUPS = ("m_q", "m_k", "m_v", "m_o", "a_q", "a_k", "a_v", "x_q", "g_m", "g_a")
_GROUP_WIDTH = dict(m_q=M_WIDTH, m_k=M_WIDTH, m_v=M_WIDTH, m_o=M_WIDTH, a_q=A_WIDTH, a_k=A_WIDTH,
                    a_v=A_WIDTH, x_q=IDX_WIDTH, g_m=D_MODEL, g_a=D_MODEL, x_k=LANES, small=LANES)
_GROUP_ORDER = _BIG_GROUPS + ("x_k", "small")
_GROUP_START = {}
_off = 0
for _g in _GROUP_ORDER:
    _GROUP_START[_g] = _off
    _off += _GROUP_WIDTH[_g]
PACKED_WIDTH = _off


def _cparams(sem, vmem=VMEM_LIMIT):
    return pltpu.CompilerParams(dimension_semantics=sem, vmem_limit_bytes=vmem)


def _layer_norm(x, g, b):
    mu = jnp.mean(x, axis=-1, keepdims=True)
    xc = x - mu
    var = jnp.mean(xc * xc, axis=-1, keepdims=True)
    return xc * lax.rsqrt(var + LN_EPS) * g + b


def _sigmoid(x):
    return 1.0 / (1.0 + jnp.exp(-x))


def _in_proj_kernel(x_ref, g_ref, b_ref, w_ref, bias_ref, cosa_ref, sina_ref, cosb_ref, sinlo_ref, sinhi_ref,
                    h0_ref, mq_ref, mk_ref, mv_ref, mo_ref, aq_ref, ak_ref, av_ref, xq_ref, gm_ref, ga_ref,
                    xk_ref, small_ref):
    h0 = _layer_norm(x_ref[...], g_ref[...], b_ref[...])
    h0_ref[...] = h0
    hb = h0.astype(BF16)

    def proj(name, c0=0, width=None):
        start = _GROUP_START[name] + c0
        width = _GROUP_WIDTH[name] if width is None else width
        y = jnp.dot(hb, w_ref[:, start:start + width], preferred_element_type=F32)
        return y + bias_ref[:, start:start + width]

    def rope_full(y):
        return y * cosa_ref[...] + pltpu.roll(y, 64, axis=1) * sina_ref[...]

    def rope_half(y):
        return (y * cosb_ref[...] + pltpu.roll(y, 96, axis=1) * sinlo_ref[...]
                + pltpu.roll(y, 32, axis=1) * sinhi_ref[...])

    mq_ref[...] = proj("m_q").astype(BF16)
    mk_ref[...] = proj("m_k").astype(BF16)
    mv_ref[...] = proj("m_v").astype(BF16)
    mo_ref[...] = _sigmoid(proj("m_o")).astype(BF16)
    gm_ref[...] = _sigmoid(proj("g_m")).astype(BF16)
    ga_ref[...] = _sigmoid(proj("g_a")).astype(BF16)
    av_ref[...] = proj("a_v").astype(BF16)
    q_scale = A_HEAD_DIM ** -0.5 * LOG2_E
    for h in range(A_HEADS):
        sl = slice(h * LANES, (h + 1) * LANES)
        aq_ref[:, sl] = (rope_full(proj("a_q", h * LANES, LANES)) * q_scale).astype(BF16)
        ak_ref[:, sl] = rope_full(proj("a_k", h * LANES, LANES)).astype(BF16)
    for c in range(IDX_WIDTH // LANES):
        sl = slice(c * LANES, (c + 1) * LANES)
        xq_ref[:, sl] = rope_half(proj("x_q", c * LANES, LANES)).astype(BF16)
    xk_ref[...] = rope_half(proj("x_k")).astype(BF16)
    small_ref[...] = proj("small")


def _rope_tables(T):
    pos = jnp.arange(T, dtype=F32)[:, None]
    half_a = A_HEAD_DIM // 2
    inv_a = ROPE_THETA ** (-jnp.arange(half_a, dtype=F32) / half_a)
    ang_a = pos * inv_a[None, :]
    cos_a, sin_a = jnp.cos(ang_a), jnp.sin(ang_a)
    cosa = jnp.concatenate([cos_a, cos_a], axis=1)
    sina = jnp.concatenate([-sin_a, sin_a], axis=1)
    half_b = IDX_DIM // 2
    inv_b = ROPE_THETA ** (-jnp.arange(half_b, dtype=F32) / half_b)
    ang_b = pos * inv_b[None, :]
    cos_b, sin_b = jnp.cos(ang_b), jnp.sin(ang_b)
    zero = jnp.zeros_like(sin_b)
    cosb = jnp.concatenate([cos_b, cos_b, cos_b, cos_b], axis=1)
    sinlo = jnp.concatenate([-sin_b, zero, -sin_b, zero], axis=1)
    sinhi = jnp.concatenate([zero, sin_b, zero, sin_b], axis=1)
    return cosa, sina, cosb, sinlo, sinhi


def _pack_in_weights(w, b):
    sizes = (M_WIDTH, M_WIDTH, M_WIDTH, M_WIDTH, M_HEADS, M_HEADS, A_WIDTH, A_WIDTH, A_WIDTH,
             IDX_WIDTH, IDX_DIM, IDX_HEADS, D_MODEL, D_MODEL)
    names = ("m_q", "m_k", "m_v", "m_o", "m_i", "m_f", "a_q", "a_k", "a_v", "x_q", "x_k", "x_w", "g_m", "g_a")
    parts, start = {}, 0
    for n, s in zip(names, sizes):
        parts[n] = (w[:, start:start + s], b[start:start + s])
        start += s
    rows = w.shape[0]

    def pad(n_cols):
        return jnp.zeros((rows, n_cols), w.dtype), jnp.zeros((n_cols,), b.dtype)

    order = [parts[n] for n in _BIG_GROUPS]
    order += [parts["x_k"], pad(LANES - IDX_DIM)]
    order += [parts["m_i"], parts["m_f"], parts["x_w"], pad(LANES - 2 * M_HEADS - IDX_HEADS)]
    wp = jnp.concatenate([o[0] for o in order], axis=1).astype(BF16)
    bp = jnp.concatenate([o[1] for o in order], axis=0).astype(F32)[None, :]
    return wp, bp


def _in_proj(x2, ln_g, ln_b, wp, bp, T, tm):
    N = x2.shape[0]
    nt = T // tm
    tables = _rope_tables(T)
    row = lambda i: (i, 0)
    const = lambda i: (0, 0)
    tab = lambda i: (i % nt, 0)
    widths = [D_MODEL, M_WIDTH, M_WIDTH, M_WIDTH, M_WIDTH, A_WIDTH, A_WIDTH, A_WIDTH, IDX_WIDTH, D_MODEL, D_MODEL,
              LANES, LANES]
    dtypes = [F32] + [BF16] * 11 + [F32]
    return pl.pallas_call(
        _in_proj_kernel,
        grid=(N // tm,),
        in_specs=[pl.BlockSpec((tm, D_MODEL), row),
                  pl.BlockSpec((1, D_MODEL), const), pl.BlockSpec((1, D_MODEL), const),
                  pl.BlockSpec((D_MODEL, PACKED_WIDTH), const, pipeline_mode=pl.Buffered(1)),
                  pl.BlockSpec((1, PACKED_WIDTH), const)]
                 + [pl.BlockSpec((tm, LANES), tab)] * 5,
        out_specs=[pl.BlockSpec((tm, wd), row) for wd in widths],
        out_shape=[jax.ShapeDtypeStruct((N, wd), dt) for wd, dt in zip(widths, dtypes)],
        compiler_params=_cparams(("parallel",)),
        name="in_proj",
    )(x2, ln_g[None, :], ln_b[None, :], wp, bp, *tables)


def _log_sigmoid(x):
    return jnp.minimum(x, 0.0) - jnp.log(1.0 + jnp.exp(-jnp.abs(x)))


def _split3(x):
    a1 = x.astype(BF16)
    r1 = x - a1.astype(F32)
    a2 = r1.astype(BF16)
    a3 = (r1 - a2.astype(F32)).astype(BF16)
    return a1, a2, a3


def _mlstm_kernel(q_ref, k_ref, v_ref, o_ref, gm_ref, gcol_ref, grow_ref, cwq_ref, cwk_ref, cbq_ref, cbk_ref,
                  ng_ref, out_ref, qprev, kprev, shift_ref, c_state, n_state, m_state, *, L):
    c = pl.program_id(2)
    halo = 8
    r = lax.broadcasted_iota(I32, (L, L), 0)
    s = lax.broadcasted_iota(I32, (L, L), 1)
    causal = r >= s

    @pl.when(c == 0)
    def _():
        qprev[...] = jnp.zeros_like(qprev)
        kprev[...] = jnp.zeros_like(kprev)
        c_state[...] = jnp.zeros_like(c_state)
        n_state[...] = jnp.zeros_like(n_state)
        m_state[...] = jnp.zeros_like(m_state)
        for j in range(1, CONV_WIDTH):
            shift_ref[j - 1] = jnp.where(r - s == j, 1.0, 0.0).astype(BF16)

    row8 = lax.broadcasted_iota(I32, (halo, M_HEAD_DIM), 0)

    def conv_silu(prev_ref, x_ref, w_ref, b_ref):
        xb = x_ref[...]
        x = xb.astype(F32)
        acc = b_ref[...] + x * w_ref[CONV_WIDTH - 1:CONV_WIDTH, :]
        prev = prev_ref[...]
        head = jnp.zeros((halo, M_HEAD_DIM), F32)
        for j in range(1, CONV_WIDTH):
            wj = w_ref[CONV_WIDTH - 1 - j:CONV_WIDTH - j, :]
            acc = acc + jnp.dot(shift_ref[j - 1], xb, preferred_element_type=F32) * wj
            head = head + jnp.where(row8 < j, pltpu.roll(prev, j, axis=0), 0.0) * wj
        acc = jnp.concatenate([acc[0:halo, :] + head, acc[halo:, :]], axis=0)
        prev_ref[...] = x[L - halo:L, :]
        return acc * _sigmoid(acc)

    qc = conv_silu(qprev, q_ref, cwq_ref, cbq_ref) * (M_HEAD_DIM ** -0.5)
    kc = conv_silu(kprev, k_ref, cwk_ref, cbk_ref)
    v = v_ref[...]

    gcol = gcol_ref[0, 0]
    grow = grow_ref[0, 0]
    i_col, lf_col = gcol[:, 0:1], _log_sigmoid(gcol[:, 1:2])
    i_row, lf_row = grow[0:1, :], _log_sigmoid(grow[1:2, :])

    tri_col = jnp.where(causal, 1.0, 0.0).astype(BF16)
    tri_row = jnp.where(r <= s, 1.0, 0.0).astype(BF16)
    b_col = sum(jnp.dot(tri_col, part, preferred_element_type=F32)
                for part in _split3(jnp.broadcast_to(lf_col, (L, LANES))))[:, 0:1]
    b_row = sum(jnp.dot(part, tri_row, preferred_element_type=F32)
                for part in _split3(jnp.broadcast_to(lf_row, (8, L))))[0:1, :]

    m_prev = m_state[...]
    dm = jnp.where(causal, b_col - b_row + i_row, -jnp.inf)
    m_inter = b_col + m_prev
    m_t = jnp.maximum(m_inter, jnp.max(dm, axis=1, keepdims=True))
    qb, kb = qc.astype(BF16), kc.astype(BF16)
    qk = lax.dot_general(qb, kb, (((1,), (1,)), ((), ())), preferred_element_type=F32)
    sm = qk * jnp.exp(dm - m_t)
    w_inter = jnp.exp(m_inter - m_t)
    cb = c_state[...].astype(BF16)
    num = (jnp.dot(sm.astype(BF16), v, preferred_element_type=F32)
           + w_inter * jnp.dot(qb, cb, preferred_element_type=F32))
    den = (jnp.sum(sm, axis=1, keepdims=True)
           + w_inter * jnp.sum(qc * n_state[...], axis=1, keepdims=True))
    hh = num / jnp.maximum(jnp.abs(den), jnp.exp(-m_t))

    b_last = b_col[L - 1:L, :]
    g_row = b_last - b_row + i_row
    m_new = jnp.maximum(b_last + m_prev, jnp.max(g_row, axis=1, keepdims=True))
    wg_col = jnp.exp(b_last - b_col + i_col - m_new)
    decay = jnp.exp(b_last + m_prev - m_new)
    kw = kc * wg_col
    kv = lax.dot_general(kw.astype(BF16), v, (((0,), (0,)), ((), ())), preferred_element_type=F32)
    c_state[...] = decay * c_state[...] + kv
    n_state[...] = decay * n_state[...] + jnp.sum(kw, axis=0, keepdims=True)
    m_state[...] = m_new

    mu = jnp.mean(hh, axis=1, keepdims=True)
    hc = hh - mu
    var = jnp.mean(hc * hc, axis=1, keepdims=True)
    hn = hc * lax.rsqrt(var + LN_EPS) * ng_ref[...]
    out_ref[...] = (hn * o_ref[...].astype(F32) * gm_ref[...].astype(F32)).astype(BF16)


def _mlstm(mq, mk, mv, mo_sig, gm_sig, small, conv_w, conv_b, norm_g, B, T, L):
    nL = T // L
    i_pre = small[:, 0:M_HEADS].reshape(B, T, M_HEADS)
    f_pre = small[:, M_HEADS:2 * M_HEADS].reshape(B, T, M_HEADS)
    gates = jnp.stack([i_pre, f_pre], axis=-1)
    gcol = gates.transpose(0, 2, 1, 3)
    grow = gates.transpose(0, 2, 3, 1)
    blk = lambda b, h, c: (b * nL + c, h)
    head = lambda b, h, c: (0, h)
    kern = functools.partial(_mlstm_kernel, L=L)
    return pl.pallas_call(
        kern,
        grid=(B, M_HEADS, nL),
        in_specs=[pl.BlockSpec((L, M_HEAD_DIM), blk)] * 5
                 + [pl.BlockSpec((1, 1, L, 2), lambda b, h, c: (b, h, c, 0)),
                    pl.BlockSpec((1, 1, 2, L), lambda b, h, c: (b, h, 0, c)),
                    pl.BlockSpec((CONV_WIDTH, M_HEAD_DIM), head),
                    pl.BlockSpec((CONV_WIDTH, M_HEAD_DIM), head),
                    pl.BlockSpec((1, M_HEAD_DIM), head),
                    pl.BlockSpec((1, M_HEAD_DIM), head),
                    pl.BlockSpec((1, M_HEAD_DIM), head)],
        out_specs=pl.BlockSpec((L, M_HEAD_DIM), blk),
        out_shape=jax.ShapeDtypeStruct((B * T, M_WIDTH), BF16),
        scratch_shapes=[pltpu.VMEM((8, M_HEAD_DIM), F32), pltpu.VMEM((8, M_HEAD_DIM), F32),
                        pltpu.VMEM((CONV_WIDTH - 1, L, L), BF16),
                        pltpu.VMEM((M_HEAD_DIM, M_HEAD_DIM), F32), pltpu.VMEM((1, M_HEAD_DIM), F32),
                        pltpu.VMEM((1, 1), F32)],
        compiler_params=_cparams(("parallel", "parallel", "arbitrary")),
        name="mlstm",
    )(mq, mk, mv, mo_sig, gm_sig, gcol, grow,
      conv_w[:, :M_WIDTH], conv_w[:, M_WIDTH:], conv_b[None, :M_WIDTH], conv_b[None, M_WIDTH:], norm_g[None, :])


def _select_kernel(xq_ref, xk_ref, xw_ref, mask_ref, sc_ref, ti_ref, *raw_refs, TQ, T, top_k):
    qi = pl.program_id(1)
    nkt = qi + 1
    SUB = 8
    NACC = 4
    kf = float(top_k)
    w_all = xw_ref[0] * ((IDX_DIM ** -0.5) * (IDX_HEADS ** -0.5))

    key_chunk = lax.broadcasted_iota(I32, (TQ, TQ), 0) // CHUNK
    qry_chunk = lax.broadcasted_iota(I32, (TQ, TQ), 1) // CHUNK
    diag_ok = key_chunk <= qry_chunk

    def fold(op, acc, tile):
        for i in range(TQ // SUB):
            acc = op(acc, tile[i * SUB:(i + 1) * SUB, :])
        return acc

    def products(kt, raw_ref):
        koff = pl.multiple_of(jnp.minimum(kt, nkt - 1) * TQ, TQ)
        kblk = xk_ref[pl.ds(koff, TQ), 0:IDX_DIM]
        for h in range(IDX_HEADS):
            qh = xq_ref[:, h * IDX_DIM:(h + 1) * IDX_DIM]
            raw_ref[h] = lax.dot_general(kblk, qh, (((1,), (1,)), ((), ())), preferred_element_type=F32)

    def combine(kt, raw_ref, mx, mn):
        kt = jnp.minimum(kt, nkt - 1)
        koff = pl.multiple_of(kt * TQ, TQ)
        acc = jnp.zeros((TQ, TQ), F32)
        for h in range(IDX_HEADS):
            acc = acc + jnp.maximum(raw_ref[h], 0.0) * w_all[h:h + 1, :]
        ok = jnp.logical_or(kt < qi, diag_ok)
        sc_ref[pl.ds(koff, TQ), :] = jnp.where(ok, acc, NEG_BIG)
        mx = fold(jnp.maximum, mx, jnp.where(ok, acc, NEG_BIG))
        mn = fold(jnp.minimum, mn, jnp.where(ok, acc, -NEG_BIG))
        return mx, mn

    def score_pair(j, carry):
        mx, mn = carry
        products(2 * j + 1, raw_refs[1])
        mx, mn = combine(2 * j, raw_refs[0], mx, mn)
        products(2 * j + 2, raw_refs[0])
        return combine(2 * j + 1, raw_refs[1], mx, mn)

    products(0, raw_refs[0])
    mx, mn = lax.fori_loop(0, (nkt + 1) // 2, score_pair,
                           (jnp.full((SUB, TQ), NEG_BIG, F32), jnp.full((SUB, TQ), -NEG_BIG, F32)))
    rep = lambda v: jnp.broadcast_to(v, (SUB, TQ))
    mx = rep(jnp.max(mx, axis=0, keepdims=True))
    mn = rep(jnp.min(mn, axis=0, keepdims=True))

    def count(pred, ref=sc_ref):
        def body(kt, accs):
            koff = pl.multiple_of(kt * TQ, TQ)
            tile = ref[pl.ds(koff, TQ), :]
            accs = list(accs)
            for i in range(TQ // SUB):
                sv = tile[i * SUB:(i + 1) * SUB, :]
                accs[i % NACC] = accs[i % NACC] + jnp.where(pred(sv, koff + i * SUB), 1.0, 0.0)
            return tuple(accs)

        accs = lax.fori_loop(0, nkt, body, tuple(jnp.zeros((SUB, TQ), F32) for _ in range(NACC)))
        tot = accs[0]
        for a in accs[1:]:
            tot = tot + a
        return rep(jnp.sum(tot, axis=0, keepdims=True))

    def any_query(flag):
        return jnp.max(jnp.where(flag, 1.0, 0.0)) > 0.0

    q_id = qi * TQ + lax.broadcasted_iota(I32, (SUB, TQ), 1)
    n_adm = ((q_id // CHUNK + 1) * CHUNK).astype(F32)
    lo0 = mn
    hi0 = mx + jnp.maximum(jnp.abs(mx), 1e-30) * 1e-3
    c_zero = count(lambda sv, off: sv >= 0.0)
    c_pos = count(lambda sv, off: sv >= F32_TINY)
    pos_side = jnp.logical_and(c_pos >= kf, lo0 < F32_TINY)
    zero_hit = jnp.logical_and(c_pos < kf, c_zero >= kf)
    neg_side = jnp.logical_and(c_zero < kf, hi0 > 0.0)
    lo1 = jnp.where(pos_side, F32_TINY, jnp.where(zero_hit, 0.0, lo0))
    clo1 = jnp.where(pos_side, c_pos, jnp.where(zero_hit, c_zero, n_adm))
    hi1 = jnp.where(zero_hit, F32_TINY, jnp.where(neg_side, 0.0, hi0))
    chi1 = jnp.where(zero_hit, c_pos, jnp.where(neg_side, c_zero, 0.0))

    def active(lo, hi, clo):
        mid = 0.5 * lo + 0.5 * hi
        return jnp.logical_and(clo > kf, jnp.logical_and(mid > lo, mid < hi)), mid

    def halve(lo, hi, clo, chi):
        act, mid = active(lo, hi, clo)
        cnt = count(lambda sv, off: sv >= mid)
        ge = cnt >= kf
        up = jnp.logical_and(act, ge)
        dn = jnp.logical_and(act, jnp.logical_not(ge))
        return (jnp.where(up, mid, lo), jnp.where(dn, mid, hi), jnp.where(up, cnt, clo), jnp.where(dn, cnt, chi))

    def cond(carry):
        return jnp.logical_and(carry[1], carry[0] < 200)

    def body(carry):
        it, _, lo, hi, clo, chi = carry
        lo, hi, clo, chi = halve(*halve(lo, hi, clo, chi))
        return it + 1, any_query(active(lo, hi, clo)[0]), lo, hi, clo, chi

    _, _, lo, hi, clo, chi = lax.while_loop(
        cond, body, (jnp.int32(0), any_query(active(lo1, hi1, clo1)[0]), lo1, hi1, clo1, chi1))

    def key_index(off):
        return (off + lax.broadcasted_iota(I32, (SUB, TQ), 0)).astype(F32)

    def write_mask(keep):
        def write(kt, _):
            koff = pl.multiple_of(kt * TQ, TQ)
            tile = sc_ref[pl.ds(koff, TQ), :]
            slabs = []
            for i in range(TQ // SUB):
                sv = tile[i * SUB:(i + 1) * SUB, :]
                slabs.append(jnp.where(keep(sv, koff + i * SUB), 1, 0))
            mask_ref[:, pl.ds(koff, TQ)] = jnp.concatenate(slabs, axis=0).T.astype(jnp.int8)
            return 0

        lax.fori_loop(0, nkt, write, 0)

    tie = clo > kf
    has_tie = any_query(tie)

    @pl.when(jnp.logical_not(has_tie))
    def _():
        write_mask(lambda sv, off: sv >= lo)

    @pl.when(has_tie)
    def _():
        need = kf - chi
        n_tie = clo - chi

        def stage(kt, _):
            koff = pl.multiple_of(kt * TQ, TQ)
            tile = sc_ref[pl.ds(koff, TQ), :]
            slabs = []
            for i in range(TQ // SUB):
                sv = tile[i * SUB:(i + 1) * SUB, :]
                slabs.append(jnp.where(jnp.logical_and(sv >= lo, sv < hi), key_index(koff + i * SUB), 2.0 * T))
            ti_ref[pl.ds(koff, TQ), :] = jnp.concatenate(slabs, axis=0)
            return 0

        lax.fori_loop(0, nkt, stage, 0)

        def jactive(jlo, jhi):
            return jnp.logical_and(tie, jhi - jlo > 1.5)

        def jcond(carry):
            return jnp.logical_and(carry[1], carry[0] < 64)

        def jbody(carry):
            it, _, jlo, jhi, cjlo, cjhi = carry
            act = jactive(jlo, jhi)
            frac = (need - cjlo) / jnp.maximum(cjhi - cjlo, 1.0)
            jint = jnp.ceil(jlo + (jhi - jlo) * frac)
            jmid = jnp.floor(0.5 * (jlo + jhi))
            even = jnp.full((SUB, TQ), (it % 2 == 0).astype(F32), F32) > 0.5
            j = jnp.where(even, jnp.clip(jint, jlo + 1.0, jhi - 1.0), jmid)
            cnt = count(lambda tv, off: tv < j, ti_ref)
            ge = cnt >= need
            up = jnp.logical_and(act, ge)
            dn = jnp.logical_and(act, jnp.logical_not(ge))
            jhi = jnp.where(up, j, jhi)
            cjhi = jnp.where(up, cnt, cjhi)
            jlo = jnp.where(dn, j, jlo)
            cjlo = jnp.where(dn, cnt, cjlo)
            return it + 1, any_query(jactive(jlo, jhi)), jlo, jhi, cjlo, cjhi

        jlo0 = jnp.zeros((SUB, TQ), F32)
        jhi0 = jnp.full((SUB, TQ), float(T), F32)
        _, _, _, jhi, _, _ = lax.while_loop(
            jcond, jbody, (jnp.int32(0), any_query(jactive(jlo0, jhi0)), jlo0, jhi0, jlo0, n_tie))
        jsel = jnp.where(tie, jhi, float(T))
        write_mask(lambda sv, off: jnp.logical_and(
            sv >= lo, jnp.logical_or(sv >= hi, key_index(off) < jsel)))

    def zero_tile(kt, _):
        mask_ref[:, pl.ds(pl.multiple_of(kt * TQ, TQ), TQ)] = jnp.zeros((TQ, TQ), jnp.int8)
        return 0

    lax.fori_loop(nkt, T // TQ, zero_tile, 0)


def _select(xq, xk, small, B, T, TQ):
    nq = T // TQ
    top_k = min(TOPK_MAX, T // 4)
    xw = small[:, 2 * M_HEADS:2 * M_HEADS + IDX_HEADS].reshape(B, T, IDX_HEADS).transpose(0, 2, 1)
    kern = functools.partial(_select_kernel, TQ=TQ, T=T, top_k=top_k)
    return pl.pallas_call(
        kern,
        grid=(B, nq),
        in_specs=[pl.BlockSpec((TQ, IDX_WIDTH), lambda b, q: (b * nq + q, 0)),
                  pl.BlockSpec((T, LANES), lambda b, q: (b, 0)),
                  pl.BlockSpec((1, IDX_HEADS, TQ), lambda b, q: (b, 0, q))],
        out_specs=pl.BlockSpec((TQ, T), lambda b, q: (b * nq + q, 0)),
        out_shape=jax.ShapeDtypeStruct((B * T, T), jnp.int8),
        scratch_shapes=[pltpu.VMEM((T, TQ), F32), pltpu.VMEM((T, TQ), F32),
                        pltpu.VMEM((IDX_HEADS, TQ, TQ), F32), pltpu.VMEM((IDX_HEADS, TQ, TQ), F32)],
        compiler_params=_cparams(("parallel", "parallel")),
        name="select",
    )(xq, xk, xw)


def _attn_kernel(q_ref, k_ref, v_ref, mask_ref, g_ref, out_ref, bias_ref, *head_refs, TQ, TK):
    vext_refs = head_refs[0:A_HEADS]
    m_refs = head_refs[A_HEADS:2 * A_HEADS]
    acc_refs = head_refs[2 * A_HEADS:3 * A_HEADS]
    qi = pl.program_id(1)
    kt = pl.program_id(2)
    nk = pl.num_programs(2)
    last = ((qi + 1) * TQ - 1) // TK

    @pl.when(kt == 0)
    def _():
        for h in range(A_HEADS):
            m_refs[h][...] = jnp.full((TQ, LANES), NEG_BIG, F32)
            acc_refs[h][...] = jnp.zeros((TQ, 2 * A_HEAD_DIM), F32)
            vext_refs[h][:, A_HEAD_DIM:] = jnp.ones((TK, A_HEAD_DIM), BF16)

    @pl.when(kt <= last)
    def _():
        bias_ref[...] = ((mask_ref[...].astype(F32) - 1.0) * (-NEG_BIG)).astype(BF16)
        for h in range(A_HEADS):
            sl = slice(h * A_HEAD_DIM, (h + 1) * A_HEAD_DIM)
            vext_refs[h][:, 0:A_HEAD_DIM] = v_ref[:, sl]
            s = lax.dot_general(q_ref[:, sl], k_ref[:, sl], (((1,), (1,)), ((), ())), preferred_element_type=F32)
            sb = s.astype(BF16) + bias_ref[...]
            m_old = m_refs[h][...]
            mx = jnp.max(sb, axis=1, keepdims=True).astype(F32)
            m_new = jnp.maximum(m_old, jnp.broadcast_to(mx, (TQ, LANES)))
            alpha = jnp.exp2(m_old - m_new)
            p = jnp.exp2(sb - m_new[:, 0:1].astype(BF16))
            pv = jnp.dot(p, vext_refs[h][...], preferred_element_type=F32)
            acc_refs[h][...] = jnp.concatenate([alpha, alpha], axis=1) * acc_refs[h][...] + pv
            m_refs[h][...] = m_new

    @pl.when(kt == nk - 1)
    def _():
        for h in range(A_HEADS):
            sl = slice(h * A_HEAD_DIM, (h + 1) * A_HEAD_DIM)
            num = acc_refs[h][:, 0:A_HEAD_DIM]
            den = acc_refs[h][:, A_HEAD_DIM:]
            out_ref[:, sl] = (num / den * g_ref[:, sl].astype(F32)).astype(BF16)


def _attn(aq, ak, av, mask, ga_sig, B, T, TQ, TK):
    nq, nk = T // TQ, T // TK

    def kv_map(b, q, k):
        return (b * nk + jnp.minimum(k, ((q + 1) * TQ - 1) // TK), 0)

    def mask_map(b, q, k):
        return (b * nq + q, jnp.minimum(k, ((q + 1) * TQ - 1) // TK))

    qmap = lambda b, q, k: (b * nq + q, 0)
    kern = functools.partial(_attn_kernel, TQ=TQ, TK=TK)
    return pl.pallas_call(
        kern,
        grid=(B, nq, nk),
        in_specs=[pl.BlockSpec((TQ, A_WIDTH), qmap),
                  pl.BlockSpec((TK, A_WIDTH), kv_map),
                  pl.BlockSpec((TK, A_WIDTH), kv_map),
                  pl.BlockSpec((TQ, TK), mask_map),
                  pl.BlockSpec((TQ, A_WIDTH), qmap)],
        out_specs=pl.BlockSpec((TQ, A_WIDTH), qmap),
        out_shape=jax.ShapeDtypeStruct((B * T, A_WIDTH), BF16),
        scratch_shapes=[pltpu.VMEM((TQ, TK), BF16)]
                       + [pltpu.VMEM((TK, 2 * A_HEAD_DIM), BF16)] * A_HEADS
                       + [pltpu.VMEM((TQ, LANES), F32)] * A_HEADS
                       + [pltpu.VMEM((TQ, 2 * A_HEAD_DIM), F32)] * A_HEADS,
        compiler_params=_cparams(("parallel", "parallel", "arbitrary")),
        name="attn",
    )(aq, ak, av, mask, ga_sig)


def _out_proj_kernel(ym_ref, ya_ref, h0_ref, wo_ref, g_ref, b_ref, wr_hi_ref, wr_lo_ref, br_ref,
                     h1_ref, gate_ref, idx_ref, rank_ref, cnt_ref, carry_ref, *, tm, alpha):
    i = pl.program_id(0)

    @pl.when(i == 0)
    def _():
        carry_ref[...] = jnp.zeros_like(carry_ref)

    merged = (ym_ref[...].astype(F32) + ya_ref[...].astype(F32)).astype(BF16)
    y = jnp.dot(merged, wo_ref[...], preferred_element_type=F32)
    h1 = _layer_norm(alpha * h0_ref[...] + y, g_ref[...], b_ref[...])
    h1_ref[...] = h1

    h_hi = h1.astype(BF16)
    h_lo = (h1 - h_hi.astype(F32)).astype(BF16)
    logits = (jnp.dot(h_hi, wr_hi_ref[...], preferred_element_type=F32)
              + jnp.dot(h_hi, wr_lo_ref[...], preferred_element_type=F32)
              + jnp.dot(h_lo, wr_hi_ref[...], preferred_element_type=F32)) + br_ref[...]
    lane = lax.broadcasted_iota(I32, (tm, LANES), 1)
    vals, hots = [], []
    idx_out = jnp.zeros((tm, LANES), I32)
    work = logits
    for r in range(TOP_K):
        mx = jnp.max(work, axis=1, keepdims=True)
        first = jnp.min(jnp.where(work == mx, lane, LANES), axis=1, keepdims=True)
        hot = lane == first
        vals.append(mx)
        hots.append(hot)
        idx_out = jnp.where(lane == r, first, idx_out)
        work = jnp.where(hot, -jnp.inf, work)
    exps = [jnp.exp(v - vals[0]) for v in vals]
    tot = exps[0] + exps[1] + exps[2] + exps[3]
    gate_out = jnp.zeros((tm, LANES), F32)
    chosen = jnp.zeros((tm, LANES), F32)
    for r in range(TOP_K):
        gate_out = jnp.where(lane == r, exps[r] / tot, gate_out)
        chosen = chosen + jnp.where(hots[r], 1.0, 0.0)

    rr = lax.broadcasted_iota(I32, (tm, tm), 0)
    cc = lax.broadcasted_iota(I32, (tm, tm), 1)
    strict = (rr > cc).astype(BF16)
    before = jnp.dot(strict, chosen.astype(BF16), preferred_element_type=F32) + carry_ref[...]
    rank_out = jnp.zeros((tm, LANES), I32)
    for r in range(TOP_K):
        rk = jnp.sum(jnp.where(hots[r], before, 0.0), axis=1, keepdims=True)
        rank_out = jnp.where(lane == r, rk.astype(I32), rank_out)
    carry_ref[...] = carry_ref[...] + jnp.sum(chosen, axis=0, keepdims=True)

    gate_ref[...] = gate_out
    idx_ref[...] = idx_out
    rank_ref[...] = rank_out
    cnt_ref[...] = carry_ref[...]


def _out_proj(ym, ya, h0, w_out_b, ln_g, ln_b, w_router, b_router, tm, alpha):
    N = ym.shape[0]
    wr = jnp.zeros((D_MODEL, LANES), F32).at[:, :N_EXPERTS].set(w_router)
    wr_hi = wr.astype(BF16)
    wr_lo = (wr - wr_hi.astype(F32)).astype(BF16)
    br = jnp.full((1, LANES), NEG_BIG, F32).at[0, :N_EXPERTS].set(b_router)
    row = lambda i: (i, 0)
    const = lambda i: (0, 0)
    kern = functools.partial(_out_proj_kernel, tm=tm, alpha=alpha)
    return pl.pallas_call(
        kern,
        grid=(N // tm,),
        in_specs=[pl.BlockSpec((tm, D_MODEL), row), pl.BlockSpec((tm, D_MODEL), row),
                  pl.BlockSpec((tm, D_MODEL), row),
                  pl.BlockSpec((D_MODEL, D_MODEL), const),
                  pl.BlockSpec((1, D_MODEL), const), pl.BlockSpec((1, D_MODEL), const),
                  pl.BlockSpec((D_MODEL, LANES), const), pl.BlockSpec((D_MODEL, LANES), const),
                  pl.BlockSpec((1, LANES), const)],
        out_specs=[pl.BlockSpec((tm, D_MODEL), row), pl.BlockSpec((tm, LANES), row),
                   pl.BlockSpec((tm, LANES), row), pl.BlockSpec((tm, LANES), row),
                   pl.BlockSpec((1, LANES), const)],
        out_shape=[jax.ShapeDtypeStruct((N, D_MODEL), F32), jax.ShapeDtypeStruct((N, LANES), F32),
                   jax.ShapeDtypeStruct((N, LANES), I32), jax.ShapeDtypeStruct((N, LANES), I32),
                   jax.ShapeDtypeStruct((1, LANES), F32)],
        scratch_shapes=[pltpu.VMEM((1, LANES), F32)],
        compiler_params=_cparams(("arbitrary",)),
        name="out_proj",
    )(ym, ya, h0, w_out_b, ln_g[None, :], ln_b[None, :], wr_hi, wr_lo, br)


def _dispatch_kernel(pos_ref, tail_ref, nused_ref, h_ref, xs_ref, zero_ref, sem, *, tm, tr, n_tiles):
    base = pl.program_id(0) * (tm * TOP_K)

    @pl.when(pl.program_id(0) == 0)
    def _():
        zero_ref[...] = jnp.zeros_like(zero_ref)

        def clear(row0):
            return pltpu.make_async_copy(zero_ref, xs_ref.at[pl.ds(pl.multiple_of(row0, tr), tr), :], sem)

        def start_tail(e, _):
            @pl.when(tail_ref[e] >= 0)
            def _():
                clear(jnp.maximum(tail_ref[e], 0)).start()
            return 0

        def wait_tail(e, _):
            @pl.when(tail_ref[e] >= 0)
            def _():
                clear(jnp.maximum(tail_ref[e], 0)).wait()
            return 0

        def start_unused(t, _):
            clear(t * tr).start()
            return 0

        def wait_unused(t, _):
            clear(t * tr).wait()
            return 0

        lax.fori_loop(0, N_EXPERTS, start_tail, 0)
        lax.fori_loop(nused_ref[0], n_tiles, start_unused, 0)
        lax.fori_loop(0, N_EXPERTS, wait_tail, 0)
        lax.fori_loop(nused_ref[0], n_tiles, wait_unused, 0)

    def copy(t, r):
        dst = pos_ref[base + t * TOP_K + r]
        return pltpu.make_async_copy(h_ref.at[pl.ds(t, 1), :], xs_ref.at[pl.ds(dst, 1), :], sem)

    def start(t, _):
        for r in range(TOP_K):
            copy(t, r).start()
        return 0

    def wait(t, _):
        for r in range(TOP_K):
            copy(t, r).wait()
        return 0

    lax.fori_loop(0, tm, start, 0)
    lax.fori_loop(0, tm, wait, 0)


def _dispatch(h1, pos_flat, tail_row, n_used, n_rows, tm, tr):
    N = h1.shape[0]
    kern = functools.partial(_dispatch_kernel, tm=tm, tr=tr, n_tiles=n_rows // tr)
    return pl.pallas_call(
        kern,
        grid_spec=pltpu.PrefetchScalarGridSpec(
            num_scalar_prefetch=3,
            grid=(N // tm,),
            in_specs=[pl.BlockSpec((tm, D_MODEL), lambda i, pos, tail, nused: (i, 0))],
            out_specs=pl.BlockSpec(memory_space=pl.ANY),
            scratch_shapes=[pltpu.VMEM((tr, D_MODEL), F32), pltpu.SemaphoreType.DMA(())]),
        out_shape=jax.ShapeDtypeStruct((n_rows, D_MODEL), F32),
        compiler_params=_cparams(("arbitrary",)),
        name="dispatch",
    )(pos_flat, tail_row, n_used, h1)


def _expert_kernel(te_ref, nused_ref, xs_ref, wg_ref, wu_ref, wd_ref, bg_ref, bu_ref, bd_ref, ys_ref,
                   wgb, wub, wdb):
    i = pl.program_id(0)
    used = i < nused_ref[0]
    fresh = jnp.logical_or(i == 0, te_ref[i] != te_ref[jnp.maximum(i - 1, 0)])

    @pl.when(jnp.logical_and(used, fresh))
    def _():
        wgb[...] = wg_ref[0].astype(BF16)
        wub[...] = wu_ref[0].astype(BF16)
        wdb[...] = wd_ref[0].astype(BF16)

    @pl.when(used)
    def _():
        x = xs_ref[...].astype(BF16)
        g = jnp.minimum(jnp.dot(x, wgb[...], preferred_element_type=F32) + bg_ref[0], SWIGLU_LIMIT)
        u = jnp.clip(jnp.dot(x, wub[...], preferred_element_type=F32) + bu_ref[0], -SWIGLU_LIMIT, SWIGLU_LIMIT)
        act = (u + 1.0) * g * _sigmoid(SWIGLU_ALPHA * g)
        ys_ref[...] = jnp.dot(act.astype(BF16), wdb[...], preferred_element_type=F32) + bd_ref[0]

    @pl.when(jnp.logical_not(used))
    def _():
        ys_ref[...] = jnp.zeros_like(ys_ref)


def _experts(xs, tile_expert, n_used, w_gate, b_gate, w_up, b_up, w_down, b_down, tr):
    P = xs.shape[0]
    wmap = lambda i, te, nu: (te[i], 0, 0)
    return pl.pallas_call(
        _expert_kernel,
        grid_spec=pltpu.PrefetchScalarGridSpec(
            num_scalar_prefetch=2,
            grid=(P // tr,),
            in_specs=[pl.BlockSpec((tr, D_MODEL), lambda i, te, nu: (i, 0)),
                      pl.BlockSpec((1, D_MODEL, D_FF), wmap), pl.BlockSpec((1, D_MODEL, D_FF), wmap),
                      pl.BlockSpec((1, D_FF, D_MODEL), wmap),
                      pl.BlockSpec((1, 1, D_FF), wmap), pl.BlockSpec((1, 1, D_FF), wmap),
                      pl.BlockSpec((1, 1, D_MODEL), wmap)],
            out_specs=pl.BlockSpec((tr, D_MODEL), lambda i, te, nu: (i, 0)),
            scratch_shapes=[pltpu.VMEM((D_MODEL, D_FF), BF16), pltpu.VMEM((D_MODEL, D_FF), BF16),
                            pltpu.VMEM((D_FF, D_MODEL), BF16)]),
        out_shape=jax.ShapeDtypeStruct((P, D_MODEL), F32),
        compiler_params=_cparams(("arbitrary",)),
        name="experts",
    )(tile_expert, n_used, xs, w_gate, w_up, w_down, b_gate[:, None, :], b_up[:, None, :], b_down[:, None, :])


def _combine_kernel(pos_ref, h1_ref, gate_ref, g_ref, b_ref, ys_ref, out_ref, buf, sems, *, tm, alpha):
    i = pl.program_id(0)
    slot = i % 2

    def copy(step, s, t, r):
        src = pos_ref[(step * tm + t) * TOP_K + r]
        return pltpu.make_async_copy(ys_ref.at[pl.ds(src, 1), :], buf.at[s, r, pl.ds(t, 1), :], sems.at[s])

    def start_tile(step, s):
        def start(t, _):
            for r in range(TOP_K):
                copy(step, s, t, r).start()
            return 0

        lax.fori_loop(0, tm, start, 0)

    def wait_tile(step, s):
        def wait(t, _):
            for r in range(TOP_K):
                copy(step, s, t, r).wait()
            return 0

        lax.fori_loop(0, tm, wait, 0)

    @pl.when(i == 0)
    def _():
        start_tile(0, 0)

    @pl.when(i + 1 < pl.num_programs(0))
    def _():
        start_tile(i + 1, 1 - slot)

    wait_tile(i, slot)
    gates = gate_ref[...]
    moe = gates[:, 0:1] * buf[slot, 0]
    for r in range(1, TOP_K):
        moe = moe + gates[:, r:r + 1] * buf[slot, r]
    out_ref[...] = _layer_norm(alpha * h1_ref[...] + moe, g_ref[...], b_ref[...])


def _combine(h1, gates, ys, pos_flat, ln_g, ln_b, tm, alpha):
    N = h1.shape[0]
    kern = functools.partial(_combine_kernel, tm=tm, alpha=alpha)
    return pl.pallas_call(
        kern,
        grid_spec=pltpu.PrefetchScalarGridSpec(
            num_scalar_prefetch=1,
            grid=(N // tm,),
            in_specs=[pl.BlockSpec((tm, D_MODEL), lambda i, pos: (i, 0)),
                      pl.BlockSpec((tm, LANES), lambda i, pos: (i, 0)),
                      pl.BlockSpec((1, D_MODEL), lambda i, pos: (0, 0)),
                      pl.BlockSpec((1, D_MODEL), lambda i, pos: (0, 0)),
                      pl.BlockSpec(memory_space=pl.ANY)],
            out_specs=pl.BlockSpec((tm, D_MODEL), lambda i, pos: (i, 0)),
            scratch_shapes=[pltpu.VMEM((2, TOP_K, tm, D_MODEL), F32), pltpu.SemaphoreType.DMA((2,))]),
        out_shape=jax.ShapeDtypeStruct((N, D_MODEL), F32),
        compiler_params=_cparams(("arbitrary",)),
        name="combine",
    )(pos_flat, h1, gates, ln_g[None, :], ln_b[None, :], ys)


def _pick(n, prefs):
    for p in prefs:
        if n % p == 0:
            return p
    raise ValueError(f"no tile size in {prefs} divides {n}")


def _moe_layout(idx, rank, counts, tr):
    tiles = (counts + tr - 1) // tr
    tile_end = jnp.cumsum(tiles)
    offs = (tile_end - tiles) * tr
    hot = idx[:, :, None] == jnp.arange(N_EXPERTS, dtype=I32)[None, None, :]
    pos = jnp.sum(jnp.where(hot, offs[None, None, :], 0), axis=-1) + rank
    n_tiles = (idx.shape[0] * TOP_K) // tr + N_EXPERTS
    tile_id = jnp.arange(n_tiles, dtype=I32)
    tile_expert = jnp.sum((tile_end[None, :] <= tile_id[:, None]).astype(I32), axis=1)
    tile_expert = jnp.minimum(tile_expert, N_EXPERTS - 1).astype(I32)
    tail_row = jnp.where(tiles > 0, (tile_end - 1) * tr, -1).astype(I32)
    return pos.reshape(-1).astype(I32), tile_expert, tile_end[-1:].astype(I32), tail_row, n_tiles * tr


def kernel(x, ln_in_g, ln_in_b, w_in, b_in, conv_w, conv_b, m_norm_g, w_out, ln1_g, ln1_b, w_router, b_router,
           w_gate, b_gate, w_up, b_up, w_down, b_down, ln2_g, ln2_b):
    B, T, D = x.shape
    depth = w_in.shape[0]
    assert D == D_MODEL and T % 256 == 0 and depth == 1
    alpha = (2.0 * depth) ** 0.25
    N = B * T
    tm_proj = _pick(T, (512, 256))
    L = 256
    TQ_SEL = 256
    TQ_ATT, TK_ATT = 512, _pick(T, (1024, 512, 256))
    tm_out = _pick(N, (512, 256))
    tm_disp = _pick(N, (512, 256))
    tm_comb = _pick(N, (256,))
    tr = 256

    h = x.reshape(N, D)
    for l in range(depth):
        wp, bp = _pack_in_weights(w_in[l], b_in[l])
        (h0, mq, mk, mv, mo_sig, aq, ak, av, xq, gm_sig, ga_sig, xk, small) = _in_proj(
            h, ln_in_g, ln_in_b, wp, bp, T, tm_proj)
        ym = _mlstm(mq, mk, mv, mo_sig, gm_sig, small, conv_w[l], conv_b[l], m_norm_g[l], B, T, L)
        mask = _select(xq, xk, small, B, T, TQ_SEL)
        ya = _attn(aq, ak, av, mask, ga_sig, B, T, TQ_ATT, TK_ATT)
        h1, gates, idx, rank, counts = _out_proj(ym, ya, h0, w_out[l].astype(BF16), ln1_g[l], ln1_b[l],
                                                 w_router[l], b_router[l], tm_out, alpha)
        pos, tile_expert, n_used, tail_row, n_rows = _moe_layout(idx[:, :TOP_K], rank[:, :TOP_K],
                                                                 counts[0, :N_EXPERTS].astype(I32), tr)
        xs = _dispatch(h1, pos, tail_row, n_used, n_rows, tm_disp, tr)
        ys = _experts(xs, tile_expert, n_used, w_gate[l], b_gate[l], w_up[l], b_up[l], w_down[l], b_down[l], tr)
        h = _combine(h1, gates, ys, pos, ln2_g[l], ln2_b[l], tm_comb, alpha)
    return h.reshape(B, T, D)
```

```python
import functools
import math

import jax
import jax.numpy as jnp
from jax import lax
from jax.experimental import pallas as pl
from jax.experimental.pallas import tpu as pltpu

F32 = jnp.float32
BF16 = jnp.bfloat16
I32 = jnp.int32

D_MODEL = 1024
CHUNK = 64
M_HEADS = 4
M_HEAD_DIM = D_MODEL // M_HEADS
M_WIDTH = M_HEADS * M_HEAD_DIM
CONV_WIDTH = 4
A_HEADS = 8
A_HEAD_DIM = D_MODEL // A_HEADS
A_WIDTH = A_HEADS * A_HEAD_DIM
IDX_HEADS = 8
IDX_DIM = 64
IDX_WIDTH = IDX_HEADS * IDX_DIM
TOPK_MAX = 256
ROPE_THETA = 10000.0
N_EXPERTS = 32
TOP_K = 4
D_FF = D_MODEL
SWIGLU_LIMIT = 7.0
SWIGLU_ALPHA = 1.702
LN_EPS = 1e-5

LANES = 128
NEG_BIG = -1e30
F32_TINY = 1.1754943508222875e-38
LOG2_E = 1.4426950408889634
VMEM_LIMIT = 56 * 1024 * 1024

_BIG_GROUPS = ("m_q", "m_k", "m_v", "m_o", "a_q", "a_k", "a_v", "x_q", "g_m", "g_a")
_GROUP_WIDTH = dict(m_q=M_WIDTH, m_k=M_WIDTH, m_v=M_WIDTH, m_o=M_WIDTH, a_q=A_WIDTH, a_k=A_WIDTH,
                    a_v=A_WIDTH, x_q=IDX_WIDTH, g_m=D_MODEL, g_a=D_MODEL, x_k=LANES, small=LANES)
_GROUP_ORDER = _BIG_GROUPS + ("x_k", "small")
_GROUP_START = {}
_off = 0
for _g in _GROUP_ORDER:
    _GROUP_START[_g] = _off
    _off += _GROUP_WIDTH[_g]
PACKED_WIDTH = _off


def _cparams(sem, vmem=VMEM_LIMIT):
    return pltpu.CompilerParams(dimension_semantics=sem, vmem_limit_bytes=vmem)


def _layer_norm(x, g, b):
    mu = jnp.mean(x, axis=-1, keepdims=True)
    xc = x - mu
    var = jnp.mean(xc * xc, axis=-1, keepdims=True)
    return xc * lax.rsqrt(var + LN_EPS) * g + b


def _sigmoid(x):
    return 1.0 / (1.0 + jnp.exp(-x))


def _in_proj_kernel(x_ref, g_ref, b_ref, w_ref, bias_ref, cosa_ref, sina_ref, cosb_ref, sinlo_ref, sinhi_ref,
                    h0_ref, mq_ref, mk_ref, mv_ref, mo_ref, aq_ref, ak_ref, av_ref, xq_ref, gm_ref, ga_ref,
                    xk_ref, small_ref, small_t_ref):
    h0 = _layer_norm(x_ref[...], g_ref[...], b_ref[...])
    h0_ref[...] = h0
    hb = h0.astype(BF16)

    def proj(name, c0=0, width=None):
        start = _GROUP_START[name] + c0
        width = _GROUP_WIDTH[name] if width is None else width
        y = jnp.dot(hb, w_ref[:, start:start + width], preferred_element_type=F32)
        return y + bias_ref[:, start:start + width]

    def rope_full(y):
        return y * cosa_ref[...] + pltpu.roll(y, 64, axis=1) * sina_ref[...]

    def rope_half(y):
        return (y * cosb_ref[...] + pltpu.roll(y, 96, axis=1) * sinlo_ref[...]
                + pltpu.roll(y, 32, axis=1) * sinhi_ref[...])

    mq_ref[...] = proj("m_q").astype(BF16)
    mk_ref[...] = proj("m_k").astype(BF16)
    mv_ref[...] = proj("m_v").astype(BF16)
    mo_ref[...] = _sigmoid(proj("m_o")).astype(BF16)
    gm_ref[...] = _sigmoid(proj("g_m")).astype(BF16)
    ga_ref[...] = _sigmoid(proj("g_a")).astype(BF16)
    av_ref[...] = proj("a_v").astype(BF16)
    q_scale = A_HEAD_DIM ** -0.5 * LOG2_E
    for h in range(A_HEADS):
        sl = slice(h * LANES, (h + 1) * LANES)
        aq_ref[:, sl] = (rope_full(proj("a_q", h * LANES, LANES)) * q_scale).astype(BF16)
        ak_ref[:, sl] = rope_full(proj("a_k", h * LANES, LANES)).astype(BF16)
    for c in range(IDX_WIDTH // LANES):
        sl = slice(c * LANES, (c + 1) * LANES)
        xq_ref[:, sl] = rope_half(proj("x_q", c * LANES, LANES)).astype(BF16)
    xk_ref[...] = rope_half(proj("x_k")).astype(BF16)
    small = proj("small")
    small_ref[...] = small
    small_t_ref[...] = small.T


def _rope_tables(T):
    pos = jnp.arange(T, dtype=F32)[:, None]
    half_a = A_HEAD_DIM // 2
    inv_a = ROPE_THETA ** (-jnp.arange(half_a, dtype=F32) / half_a)
    ang_a = pos * inv_a[None, :]
    cos_a, sin_a = jnp.cos(ang_a), jnp.sin(ang_a)
    cosa = jnp.concatenate([cos_a, cos_a], axis=1)
    sina = jnp.concatenate([-sin_a, sin_a], axis=1)
    half_b = IDX_DIM // 2
    inv_b = ROPE_THETA ** (-jnp.arange(half_b, dtype=F32) / half_b)
    ang_b = pos * inv_b[None, :]
    cos_b, sin_b = jnp.cos(ang_b), jnp.sin(ang_b)
    zero = jnp.zeros_like(sin_b)
    cosb = jnp.concatenate([cos_b, cos_b, cos_b, cos_b], axis=1)
    sinlo = jnp.concatenate([-sin_b, zero, -sin_b, zero], axis=1)
    sinhi = jnp.concatenate([zero, sin_b, zero, sin_b], axis=1)
    return cosa, sina, cosb, sinlo, sinhi


def _pack_in_weights(w, b):
    sizes = (M_WIDTH, M_WIDTH, M_WIDTH, M_WIDTH, M_HEADS, M_HEADS, A_WIDTH, A_WIDTH, A_WIDTH,
             IDX_WIDTH, IDX_DIM, IDX_HEADS, D_MODEL, D_MODEL)
    names = ("m_q", "m_k", "m_v", "m_o", "m_i", "m_f", "a_q", "a_k", "a_v", "x_q", "x_k", "x_w", "g_m", "g_a")
    parts, start = {}, 0
    for n, s in zip(names, sizes):
        parts[n] = (w[:, start:start + s], b[start:start + s])
        start += s
    rows = w.shape[0]

    def pad(n_cols):
        return jnp.zeros((rows, n_cols), w.dtype), jnp.zeros((n_cols,), b.dtype)

    order = [parts[n] for n in _BIG_GROUPS]
    order += [parts["x_k"], pad(LANES - IDX_DIM)]
    order += [parts["m_i"], parts["m_f"], parts["x_w"], pad(LANES - 2 * M_HEADS - IDX_HEADS)]
    wp = jnp.concatenate([o[0] for o in order], axis=1).astype(BF16)
    bp = jnp.concatenate([o[1] for o in order], axis=0).astype(F32)[None, :]
    return wp, bp


def _in_proj(x2, ln_g, ln_b, wp, bp, T, tm):
    N = x2.shape[0]
    nt = T // tm
    tables = _rope_tables(T)
    row = lambda i: (i, 0)
    const = lambda i: (0, 0)
    tab = lambda i: (i % nt, 0)
    widths = [D_MODEL, M_WIDTH, M_WIDTH, M_WIDTH, M_WIDTH, A_WIDTH, A_WIDTH, A_WIDTH, IDX_WIDTH, D_MODEL, D_MODEL,
              LANES, LANES]
    dtypes = [F32] + [BF16] * 11 + [F32]
    return pl.pallas_call(
        _in_proj_kernel,
        grid=(N // tm,),
        in_specs=[pl.BlockSpec((tm, D_MODEL), row),
                  pl.BlockSpec((1, D_MODEL), const), pl.BlockSpec((1, D_MODEL), const),
                  pl.BlockSpec((D_MODEL, PACKED_WIDTH), const, pipeline_mode=pl.Buffered(1)),
                  pl.BlockSpec((1, PACKED_WIDTH), const)]
                 + [pl.BlockSpec((tm, LANES), tab)] * 5,
        out_specs=[pl.BlockSpec((tm, wd), row) for wd in widths] + [pl.BlockSpec((LANES, tm), lambda i: (0, i))],
        out_shape=[jax.ShapeDtypeStruct((N, wd), dt) for wd, dt in zip(widths, dtypes)]
                  + [jax.ShapeDtypeStruct((LANES, N), F32)],
        compiler_params=_cparams(("parallel",)),
        name="in_proj",
    )(x2, ln_g[None, :], ln_b[None, :], wp, bp, *tables)


def _log_sigmoid(x):
    return jnp.minimum(x, 0.0) - jnp.log(1.0 + jnp.exp(-jnp.abs(x)))


def _split3(x):
    a1 = x.astype(BF16)
    r1 = x - a1.astype(F32)
    a2 = r1.astype(BF16)
    a3 = (r1 - a2.astype(F32)).astype(BF16)
    return a1, a2, a3


def _mlstm_kernel(q_ref, k_ref, v_ref, o_ref, gm_ref, gcol_ref, grow_ref, cwq_ref, cwk_ref, cbq_ref, cbk_ref,
                  ng_ref, out_ref, qprev, kprev, shift_ref, c_state, n_state, m_state, *, L):
    c = pl.program_id(2)
    halo = 8
    r = lax.broadcasted_iota(I32, (L, L), 0)
    s = lax.broadcasted_iota(I32, (L, L), 1)
    causal = r >= s

    @pl.when(c == 0)
    def _():
        qprev[...] = jnp.zeros_like(qprev)
        kprev[...] = jnp.zeros_like(kprev)
        c_state[...] = jnp.zeros_like(c_state)
        n_state[...] = jnp.zeros_like(n_state)
        m_state[...] = jnp.zeros_like(m_state)
        for j in range(1, CONV_WIDTH):
            shift_ref[j - 1] = jnp.where(r - s == j, 1.0, 0.0).astype(BF16)

    row8 = lax.broadcasted_iota(I32, (halo, M_HEAD_DIM), 0)

    def conv_silu(prev_ref, x_ref, w_ref, b_ref):
        xb = x_ref[...]
        x = xb.astype(F32)
        acc = b_ref[...] + x * w_ref[CONV_WIDTH - 1:CONV_WIDTH, :]
        prev = prev_ref[...]
        head = jnp.zeros((halo, M_HEAD_DIM), F32)
        for j in range(1, CONV_WIDTH):
            wj = w_ref[CONV_WIDTH - 1 - j:CONV_WIDTH - j, :]
            acc = acc + jnp.dot(shift_ref[j - 1], xb, preferred_element_type=F32) * wj
            head = head + jnp.where(row8 < j, pltpu.roll(prev, j, axis=0), 0.0) * wj
        acc = jnp.concatenate([acc[0:halo, :] + head, acc[halo:, :]], axis=0)
        prev_ref[...] = x[L - halo:L, :]
        return acc * _sigmoid(acc)

    qc = conv_silu(qprev, q_ref, cwq_ref, cbq_ref) * (M_HEAD_DIM ** -0.5)
    kc = conv_silu(kprev, k_ref, cwk_ref, cbk_ref)
    v = v_ref[...]

    hd = pl.program_id(1)
    gcol = gcol_ref[...]
    lane = lax.broadcasted_iota(I32, (L, LANES), 1)
    i_col = jnp.sum(jnp.where(lane == hd, gcol, 0.0), axis=1, keepdims=True)
    lf_col = _log_sigmoid(jnp.sum(jnp.where(lane == hd + M_HEADS, gcol, 0.0), axis=1, keepdims=True))
    i_row = grow_ref[pl.ds(hd, 1), :]
    lf_row = _log_sigmoid(grow_ref[pl.ds(hd + M_HEADS, 1), :])

    tri_col = jnp.where(causal, 1.0, 0.0).astype(BF16)
    tri_row = jnp.where(r <= s, 1.0, 0.0).astype(BF16)
    b_col = sum(jnp.dot(tri_col, part, preferred_element_type=F32)
                for part in _split3(jnp.broadcast_to(lf_col, (L, LANES))))[:, 0:1]
    b_row = sum(jnp.dot(part, tri_row, preferred_element_type=F32)
                for part in _split3(jnp.broadcast_to(lf_row, (8, L))))[0:1, :]

    m_prev = m_state[...]
    dm = jnp.where(causal, b_col - b_row + i_row, -jnp.inf)
    m_inter = b_col + m_prev
    m_t = jnp.maximum(m_inter, jnp.max(dm, axis=1, keepdims=True))
    qb, kb = qc.astype(BF16), kc.astype(BF16)
    qk = lax.dot_general(qb, kb, (((1,), (1,)), ((), ())), preferred_element_type=F32)
    sm = qk * jnp.exp(dm - m_t)
    w_inter = jnp.exp(m_inter - m_t)
    cb = c_state[...].astype(BF16)
    num = (jnp.dot(sm.astype(BF16), v, preferred_element_type=F32)
           + w_inter * jnp.dot(qb, cb, preferred_element_type=F32))
    den = (jnp.sum(sm, axis=1, keepdims=True)
           + w_inter * jnp.sum(qc * n_state[...], axis=1, keepdims=True))
    hh = num / jnp.maximum(jnp.abs(den), jnp.exp(-m_t))

    b_last = b_col[L - 1:L, :]
    g_row = b_last - b_row + i_row
    m_new = jnp.maximum(b_last + m_prev, jnp.max(g_row, axis=1, keepdims=True))
    wg_col = jnp.exp(b_last - b_col + i_col - m_new)
    decay = jnp.exp(b_last + m_prev - m_new)
    kw = kc * wg_col
    kv = lax.dot_general(kw.astype(BF16), v, (((0,), (0,)), ((), ())), preferred_element_type=F32)
    c_state[...] = decay * c_state[...] + kv
    n_state[...] = decay * n_state[...] + jnp.sum(kw, axis=0, keepdims=True)
    m_state[...] = m_new

    mu = jnp.mean(hh, axis=1, keepdims=True)
    hc = hh - mu
    var = jnp.mean(hc * hc, axis=1, keepdims=True)
    hn = hc * lax.rsqrt(var + LN_EPS) * ng_ref[...]
    out_ref[...] = (hn * o_ref[...].astype(F32) * gm_ref[...].astype(F32)).astype(BF16)


def _mlstm(mq, mk, mv, mo_sig, gm_sig, small, small_t, conv_w, conv_b, norm_g, B, T, L):
    nL = T // L
    blk = lambda b, h, c: (b * nL + c, h)
    head = lambda b, h, c: (0, h)
    kern = functools.partial(_mlstm_kernel, L=L)
    return pl.pallas_call(
        kern,
        grid=(B, M_HEADS, nL),
        in_specs=[pl.BlockSpec((L, M_HEAD_DIM), blk)] * 5
                 + [pl.BlockSpec((L, LANES), lambda b, h, c: (b * nL + c, 0)),
                    pl.BlockSpec((2 * M_HEADS, L), lambda b, h, c: (0, b * nL + c)),
                    pl.BlockSpec((CONV_WIDTH, M_HEAD_DIM), head),
                    pl.BlockSpec((CONV_WIDTH, M_HEAD_DIM), head),
                    pl.BlockSpec((1, M_HEAD_DIM), head),
                    pl.BlockSpec((1, M_HEAD_DIM), head),
                    pl.BlockSpec((1, M_HEAD_DIM), head)],
        out_specs=pl.BlockSpec((L, M_HEAD_DIM), blk),
        out_shape=jax.ShapeDtypeStruct((B * T, M_WIDTH), BF16),
        scratch_shapes=[pltpu.VMEM((8, M_HEAD_DIM), F32), pltpu.VMEM((8, M_HEAD_DIM), F32),
                        pltpu.VMEM((CONV_WIDTH - 1, L, L), BF16),
                        pltpu.VMEM((M_HEAD_DIM, M_HEAD_DIM), F32), pltpu.VMEM((1, M_HEAD_DIM), F32),
                        pltpu.VMEM((1, 1), F32)],
        compiler_params=_cparams(("parallel", "parallel", "arbitrary")),
        name="mlstm",
    )(mq, mk, mv, mo_sig, gm_sig, small, small_t,
      conv_w[:, :M_WIDTH], conv_w[:, M_WIDTH:], conv_b[None, :M_WIDTH], conv_b[None, M_WIDTH:], norm_g[None, :])


def _select_kernel(xq_ref, xk_ref, xw_ref, mask_ref, sc_ref, ti_ref, *raw_refs, TQ, T, top_k):
    qi = pl.program_id(1)
    nkt = qi + 1
    SUB = 8
    NACC = 4
    kf = float(top_k)
    w_all = xw_ref[...] * ((IDX_DIM ** -0.5) * (IDX_HEADS ** -0.5))

    key_chunk = lax.broadcasted_iota(I32, (TQ, TQ), 0) // CHUNK
    qry_chunk = lax.broadcasted_iota(I32, (TQ, TQ), 1) // CHUNK
    diag_ok = key_chunk <= qry_chunk

    def fold(op, acc, tile):
        for i in range(TQ // SUB):
            acc = op(acc, tile[i * SUB:(i + 1) * SUB, :])
        return acc

    def products(kt, raw_ref):
        koff = pl.multiple_of(jnp.minimum(kt, nkt - 1) * TQ, TQ)
        kblk = xk_ref[pl.ds(koff, TQ), 0:IDX_DIM]
        for h in range(IDX_HEADS):
            qh = xq_ref[:, h * IDX_DIM:(h + 1) * IDX_DIM]
            raw_ref[h] = lax.dot_general(kblk, qh, (((1,), (1,)), ((), ())), preferred_element_type=F32)

    def combine(kt, raw_ref, mx, mn):
        kt = jnp.minimum(kt, nkt - 1)
        koff = pl.multiple_of(kt * TQ, TQ)
        acc = jnp.zeros((TQ, TQ), F32)
        for h in range(IDX_HEADS):
            acc = acc + jnp.maximum(raw_ref[h], 0.0) * w_all[h:h + 1, :]
        ok = jnp.logical_or(kt < qi, diag_ok)
        sc_ref[pl.ds(koff, TQ), :] = jnp.where(ok, acc, NEG_BIG)
        mx = fold(jnp.maximum, mx, jnp.where(ok, acc, NEG_BIG))
        mn = fold(jnp.minimum, mn, jnp.where(ok, acc, -NEG_BIG))
        return mx, mn

    def score_pair(j, carry):
        mx, mn = carry
        products(2 * j + 1, raw_refs[1])
        mx, mn = combine(2 * j, raw_refs[0], mx, mn)
        products(2 * j + 2, raw_refs[0])
        return combine(2 * j + 1, raw_refs[1], mx, mn)

    products(0, raw_refs[0])
    mx, mn = lax.fori_loop(0, (nkt + 1) // 2, score_pair,
                           (jnp.full((SUB, TQ), NEG_BIG, F32), jnp.full((SUB, TQ), -NEG_BIG, F32)))
    rep = lambda v: jnp.broadcast_to(v, (SUB, TQ))
    mx = rep(jnp.max(mx, axis=0, keepdims=True))
    mn = rep(jnp.min(mn, axis=0, keepdims=True))

    def count(pred, ref=sc_ref):
        def body(kt, accs):
            koff = pl.multiple_of(kt * TQ, TQ)
            tile = ref[pl.ds(koff, TQ), :]
            accs = list(accs)
            for i in range(TQ // SUB):
                sv = tile[i * SUB:(i + 1) * SUB, :]
                accs[i % NACC] = accs[i % NACC] + jnp.where(pred(sv, koff + i * SUB), 1.0, 0.0)
            return tuple(accs)

        accs = lax.fori_loop(0, nkt, body, tuple(jnp.zeros((SUB, TQ), F32) for _ in range(NACC)))
        tot = accs[0]
        for a in accs[1:]:
            tot = tot + a
        return rep(jnp.sum(tot, axis=0, keepdims=True))

    def any_query(flag):
        return jnp.max(jnp.where(flag, 1.0, 0.0)) > 0.0

    q_id = qi * TQ + lax.broadcasted_iota(I32, (SUB, TQ), 1)
    n_adm = ((q_id // CHUNK + 1) * CHUNK).astype(F32)
    lo0 = mn
    hi0 = mx + jnp.maximum(jnp.abs(mx), 1e-30) * 1e-3
    c_zero = count(lambda sv, off: sv >= 0.0)
    c_pos = count(lambda sv, off: sv >= F32_TINY)
    pos_side = jnp.logical_and(c_pos >= kf, lo0 < F32_TINY)
    zero_hit = jnp.logical_and(c_pos < kf, c_zero >= kf)
    neg_side = jnp.logical_and(c_zero < kf, hi0 > 0.0)
    lo1 = jnp.where(pos_side, F32_TINY, jnp.where(zero_hit, 0.0, lo0))
    clo1 = jnp.where(pos_side, c_pos, jnp.where(zero_hit, c_zero, n_adm))
    hi1 = jnp.where(zero_hit, F32_TINY, jnp.where(neg_side, 0.0, hi0))
    chi1 = jnp.where(zero_hit, c_pos, jnp.where(neg_side, c_zero, 0.0))

    def active(lo, hi, clo):
        mid = 0.5 * lo + 0.5 * hi
        return jnp.logical_and(clo > kf, jnp.logical_and(mid > lo, mid < hi)), mid

    def halve(lo, hi, clo, chi):
        act, mid = active(lo, hi, clo)
        cnt = count(lambda sv, off: sv >= mid)
        ge = cnt >= kf
        up = jnp.logical_and(act, ge)
        dn = jnp.logical_and(act, jnp.logical_not(ge))
        return (jnp.where(up, mid, lo), jnp.where(dn, mid, hi), jnp.where(up, cnt, clo), jnp.where(dn, cnt, chi))

    def cond(carry):
        return jnp.logical_and(carry[1], carry[0] < 200)

    def body(carry):
        it, _, lo, hi, clo, chi = carry
        lo, hi, clo, chi = halve(*halve(lo, hi, clo, chi))
        return it + 1, any_query(active(lo, hi, clo)[0]), lo, hi, clo, chi

    _, _, lo, hi, clo, chi = lax.while_loop(
        cond, body, (jnp.int32(0), any_query(active(lo1, hi1, clo1)[0]), lo1, hi1, clo1, chi1))

    def key_index(off):
        return (off + lax.broadcasted_iota(I32, (SUB, TQ), 0)).astype(F32)

    def write_mask(keep):
        def write(kt, _):
            koff = pl.multiple_of(kt * TQ, TQ)
            tile = sc_ref[pl.ds(koff, TQ), :]
            slabs = []
            for i in range(TQ // SUB):
                sv = tile[i * SUB:(i + 1) * SUB, :]
                slabs.append(jnp.where(keep(sv, koff + i * SUB), 1, 0))
            mask_ref[:, pl.ds(koff, TQ)] = jnp.concatenate(slabs, axis=0).T.astype(jnp.int8)
            return 0

        lax.fori_loop(0, nkt, write, 0)

    tie = clo > kf
    has_tie = any_query(tie)

    @pl.when(jnp.logical_not(has_tie))
    def _():
        write_mask(lambda sv, off: sv >= lo)

    @pl.when(has_tie)
    def _():
        need = kf - chi
        n_tie = clo - chi

        def stage(kt, _):
            koff = pl.multiple_of(kt * TQ, TQ)
            tile = sc_ref[pl.ds(koff, TQ), :]
            slabs = []
            for i in range(TQ // SUB):
                sv = tile[i * SUB:(i + 1) * SUB, :]
                slabs.append(jnp.where(jnp.logical_and(sv >= lo, sv < hi), key_index(koff + i * SUB), 2.0 * T))
            ti_ref[pl.ds(koff, TQ), :] = jnp.concatenate(slabs, axis=0)
            return 0

        lax.fori_loop(0, nkt, stage, 0)

        def jactive(jlo, jhi):
            return jnp.logical_and(tie, jhi - jlo > 1.5)

        def jcond(carry):
            return jnp.logical_and(carry[1], carry[0] < 64)

        def jbody(carry):
            it, _, jlo, jhi, cjlo, cjhi = carry
            act = jactive(jlo, jhi)
            frac = (need - cjlo) / jnp.maximum(cjhi - cjlo, 1.0)
            jint = jnp.ceil(jlo + (jhi - jlo) * frac)
            jmid = jnp.floor(0.5 * (jlo + jhi))
            even = jnp.full((SUB, TQ), (it % 2 == 0).astype(F32), F32) > 0.5
            j = jnp.where(even, jnp.clip(jint, jlo + 1.0, jhi - 1.0), jmid)
            cnt = count(lambda tv, off: tv < j, ti_ref)
            ge = cnt >= need
            up = jnp.logical_and(act, ge)
            dn = jnp.logical_and(act, jnp.logical_not(ge))
            jhi = jnp.where(up, j, jhi)
            cjhi = jnp.where(up, cnt, cjhi)
            jlo = jnp.where(dn, j, jlo)
            cjlo = jnp.where(dn, cnt, cjlo)
            return it + 1, any_query(jactive(jlo, jhi)), jlo, jhi, cjlo, cjhi

        jlo0 = jnp.zeros((SUB, TQ), F32)
        jhi0 = jnp.full((SUB, TQ), float(T), F32)
        _, _, _, jhi, _, _ = lax.while_loop(
            jcond, jbody, (jnp.int32(0), any_query(jactive(jlo0, jhi0)), jlo0, jhi0, jlo0, n_tie))
        jsel = jnp.where(tie, jhi, float(T))
        write_mask(lambda sv, off: jnp.logical_and(
            sv >= lo, jnp.logical_or(sv >= hi, key_index(off) < jsel)))

    def zero_tile(kt, _):
        mask_ref[:, pl.ds(pl.multiple_of(kt * TQ, TQ), TQ)] = jnp.zeros((TQ, TQ), jnp.int8)
        return 0

    lax.fori_loop(nkt, T // TQ, zero_tile, 0)


def _select(xq, xk, small_t, B, T, TQ):
    nq = T // TQ
    top_k = min(TOPK_MAX, T // 4)
    assert 2 * M_HEADS == IDX_HEADS
    kern = functools.partial(_select_kernel, TQ=TQ, T=T, top_k=top_k)
    return pl.pallas_call(
        kern,
        grid=(B, nq),
        in_specs=[pl.BlockSpec((TQ, IDX_WIDTH), lambda b, q: (b * nq + q, 0)),
                  pl.BlockSpec((T, LANES), lambda b, q: (b, 0)),
                  pl.BlockSpec((IDX_HEADS, TQ), lambda b, q: (1, b * nq + q))],
        out_specs=pl.BlockSpec((TQ, T), lambda b, q: (b * nq + q, 0)),
        out_shape=jax.ShapeDtypeStruct((B * T, T), jnp.int8),
        scratch_shapes=[pltpu.VMEM((T, TQ), F32), pltpu.VMEM((T, TQ), F32),
                        pltpu.VMEM((IDX_HEADS, TQ, TQ), F32), pltpu.VMEM((IDX_HEADS, TQ, TQ), F32)],
        compiler_params=_cparams(("parallel", "parallel")),
        name="select",
    )(xq, xk, small_t)


def _attn_kernel(q_ref, k_ref, v_ref, mask_ref, g_ref, out_ref, bias_ref, *head_refs, TQ, TK):
    vext_refs = head_refs[0:A_HEADS]
    m_refs = head_refs[A_HEADS:2 * A_HEADS]
    acc_refs = head_refs[2 * A_HEADS:3 * A_HEADS]
    qi = pl.program_id(1)
    kt = pl.program_id(2)
    nk = pl.num_programs(2)
    last = ((qi + 1) * TQ - 1) // TK

    @pl.when(kt == 0)
    def _():
        for h in range(A_HEADS):
            m_refs[h][...] = jnp.full((TQ, LANES), NEG_BIG, F32)
            acc_refs[h][...] = jnp.zeros((TQ, 2 * A_HEAD_DIM), F32)
            vext_refs[h][:, A_HEAD_DIM:] = jnp.ones((TK, A_HEAD_DIM), BF16)

    @pl.when(kt <= last)
    def _():
        bias_ref[...] = ((mask_ref[...].astype(F32) - 1.0) * (-NEG_BIG)).astype(BF16)
        for h in range(A_HEADS):
            sl = slice(h * A_HEAD_DIM, (h + 1) * A_HEAD_DIM)
            vext_refs[h][:, 0:A_HEAD_DIM] = v_ref[:, sl]
            s = lax.dot_general(q_ref[:, sl], k_ref[:, sl], (((1,), (1,)), ((), ())), preferred_element_type=F32)
            sb = s.astype(BF16) + bias_ref[...]
            m_old = m_refs[h][...]
            mx = jnp.max(sb, axis=1, keepdims=True).astype(F32)
            m_new = jnp.maximum(m_old, jnp.broadcast_to(mx, (TQ, LANES)))
            alpha = jnp.exp2(m_old - m_new)
            p = jnp.exp2(sb - m_new[:, 0:1].astype(BF16))
            pv = jnp.dot(p, vext_refs[h][...], preferred_element_type=F32)
            acc_refs[h][...] = jnp.concatenate([alpha, alpha], axis=1) * acc_refs[h][...] + pv
            m_refs[h][...] = m_new

    @pl.when(kt == nk - 1)
    def _():
        for h in range(A_HEADS):
            sl = slice(h * A_HEAD_DIM, (h + 1) * A_HEAD_DIM)
            num = acc_refs[h][:, 0:A_HEAD_DIM]
            den = acc_refs[h][:, A_HEAD_DIM:]
            out_ref[:, sl] = (num / den * g_ref[:, sl].astype(F32)).astype(BF16)


def _attn(aq, ak, av, mask, ga_sig, B, T, TQ, TK):
    nq, nk = T // TQ, T // TK

    def kv_map(b, q, k):
        return (b * nk + jnp.minimum(k, ((q + 1) * TQ - 1) // TK), 0)

    def mask_map(b, q, k):
        return (b * nq + q, jnp.minimum(k, ((q + 1) * TQ - 1) // TK))

    qmap = lambda b, q, k: (b * nq + q, 0)
    kern = functools.partial(_attn_kernel, TQ=TQ, TK=TK)
    return pl.pallas_call(
        kern,
        grid=(B, nq, nk),
        in_specs=[pl.BlockSpec((TQ, A_WIDTH), qmap),
                  pl.BlockSpec((TK, A_WIDTH), kv_map),
                  pl.BlockSpec((TK, A_WIDTH), kv_map),
                  pl.BlockSpec((TQ, TK), mask_map),
                  pl.BlockSpec((TQ, A_WIDTH), qmap)],
        out_specs=pl.BlockSpec((TQ, A_WIDTH), qmap),
        out_shape=jax.ShapeDtypeStruct((B * T, A_WIDTH), BF16),
        scratch_shapes=[pltpu.VMEM((TQ, TK), BF16)]
                       + [pltpu.VMEM((TK, 2 * A_HEAD_DIM), BF16)] * A_HEADS
                       + [pltpu.VMEM((TQ, LANES), F32)] * A_HEADS
                       + [pltpu.VMEM((TQ, 2 * A_HEAD_DIM), F32)] * A_HEADS,
        compiler_params=_cparams(("parallel", "parallel", "arbitrary")),
        name="attn",
    )(aq, ak, av, mask, ga_sig)


def _out_proj_kernel(ym_ref, ya_ref, h0_ref, wo_ref, g_ref, b_ref, wr_hi_ref, wr_lo_ref, br_ref,
                     h1_ref, gate_ref, idx_ref, rank_ref, cnt_ref, carry_ref, *, tm, alpha):
    i = pl.program_id(0)

    @pl.when(i == 0)
    def _():
        carry_ref[...] = jnp.zeros_like(carry_ref)

    merged = (ym_ref[...].astype(F32) + ya_ref[...].astype(F32)).astype(BF16)
    y = jnp.dot(merged, wo_ref[...], preferred_element_type=F32)
    h1 = _layer_norm(alpha * h0_ref[...] + y, g_ref[...], b_ref[...])
    h1_ref[...] = h1

    h_hi = h1.astype(BF16)
    h_lo = (h1 - h_hi.astype(F32)).astype(BF16)
    logits = (jnp.dot(h_hi, wr_hi_ref[...], preferred_element_type=F32)
              + jnp.dot(h_hi, wr_lo_ref[...], preferred_element_type=F32)
              + jnp.dot(h_lo, wr_hi_ref[...], preferred_element_type=F32)) + br_ref[...]
    lane = lax.broadcasted_iota(I32, (tm, LANES), 1)
    vals, hots = [], []
    idx_out = jnp.zeros((tm, LANES), I32)
    work = logits
    for r in range(TOP_K):
        mx = jnp.max(work, axis=1, keepdims=True)
        first = jnp.min(jnp.where(work == mx, lane, LANES), axis=1, keepdims=True)
        hot = lane == first
        vals.append(mx)
        hots.append(hot)
        idx_out = jnp.where(lane == r, first, idx_out)
        work = jnp.where(hot, -jnp.inf, work)
    exps = [jnp.exp(v - vals[0]) for v in vals]
    tot = exps[0] + exps[1] + exps[2] + exps[3]
    gate_out = jnp.zeros((tm, LANES), F32)
    chosen = jnp.zeros((tm, LANES), F32)
    for r in range(TOP_K):
        gate_out = jnp.where(lane == r, exps[r] / tot, gate_out)
        chosen = chosen + jnp.where(hots[r], 1.0, 0.0)

    rr = lax.broadcasted_iota(I32, (tm, tm), 0)
    cc = lax.broadcasted_iota(I32, (tm, tm), 1)
    strict = (rr > cc).astype(BF16)
    before = jnp.dot(strict, chosen.astype(BF16), preferred_element_type=F32) + carry_ref[...]
    rank_out = jnp.zeros((tm, LANES), I32)
    for r in range(TOP_K):
        rk = jnp.sum(jnp.where(hots[r], before, 0.0), axis=1, keepdims=True)
        rank_out = jnp.where(lane == r, rk.astype(I32), rank_out)
    carry_ref[...] = carry_ref[...] + jnp.sum(chosen, axis=0, keepdims=True)

    gate_ref[...] = gate_out
    idx_ref[...] = idx_out
    rank_ref[...] = rank_out
    cnt_ref[...] = carry_ref[...]


def _out_proj(ym, ya, h0, w_out_b, ln_g, ln_b, w_router, b_router, tm, alpha):
    N = ym.shape[0]
    wr = jnp.zeros((D_MODEL, LANES), F32).at[:, :N_EXPERTS].set(w_router)
    wr_hi = wr.astype(BF16)
    wr_lo = (wr - wr_hi.astype(F32)).astype(BF16)
    br = jnp.full((1, LANES), NEG_BIG, F32).at[0, :N_EXPERTS].set(b_router)
    row = lambda i: (i, 0)
    const = lambda i: (0, 0)
    kern = functools.partial(_out_proj_kernel, tm=tm, alpha=alpha)
    return pl.pallas_call(
        kern,
        grid=(N // tm,),
        in_specs=[pl.BlockSpec((tm, D_MODEL), row), pl.BlockSpec((tm, D_MODEL), row),
                  pl.BlockSpec((tm, D_MODEL), row),
                  pl.BlockSpec((D_MODEL, D_MODEL), const),
                  pl.BlockSpec((1, D_MODEL), const), pl.BlockSpec((1, D_MODEL), const),
                  pl.BlockSpec((D_MODEL, LANES), const), pl.BlockSpec((D_MODEL, LANES), const),
                  pl.BlockSpec((1, LANES), const)],
        out_specs=[pl.BlockSpec((tm, D_MODEL), row), pl.BlockSpec((tm, LANES), row),
                   pl.BlockSpec((tm, LANES), row), pl.BlockSpec((tm, LANES), row),
                   pl.BlockSpec((1, LANES), const)],
        out_shape=[jax.ShapeDtypeStruct((N, D_MODEL), F32), jax.ShapeDtypeStruct((N, LANES), F32),
                   jax.ShapeDtypeStruct((N, LANES), I32), jax.ShapeDtypeStruct((N, LANES), I32),
                   jax.ShapeDtypeStruct((1, LANES), F32)],
        scratch_shapes=[pltpu.VMEM((1, LANES), F32)],
        compiler_params=_cparams(("arbitrary",)),
        name="out_proj",
    )(ym, ya, h0, w_out_b, ln_g[None, :], ln_b[None, :], wr_hi, wr_lo, br)


def _dispatch_kernel(pos_ref, tail_ref, nused_ref, h_ref, xs_ref, zero_ref, sem, *, tm, tr, n_tiles):
    base = pl.program_id(0) * (tm * TOP_K)

    @pl.when(pl.program_id(0) == 0)
    def _():
        zero_ref[...] = jnp.zeros_like(zero_ref)

        def clear(row0):
            return pltpu.make_async_copy(zero_ref, xs_ref.at[pl.ds(pl.multiple_of(row0, tr), tr), :], sem)

        def start_tail(e, _):
            @pl.when(tail_ref[e] >= 0)
            def _():
                clear(jnp.maximum(tail_ref[e], 0)).start()
            return 0

        def wait_tail(e, _):
            @pl.when(tail_ref[e] >= 0)
            def _():
                clear(jnp.maximum(tail_ref[e], 0)).wait()
            return 0

        def start_unused(t, _):
            clear(t * tr).start()
            return 0

        def wait_unused(t, _):
            clear(t * tr).wait()
            return 0

        lax.fori_loop(0, N_EXPERTS, start_tail, 0)
        lax.fori_loop(nused_ref[0], n_tiles, start_unused, 0)
        lax.fori_loop(0, N_EXPERTS, wait_tail, 0)
        lax.fori_loop(nused_ref[0], n_tiles, wait_unused, 0)

    def copy(t, r):
        dst = pos_ref[base + t * TOP_K + r]
        return pltpu.make_async_copy(h_ref.at[pl.ds(t, 1), :], xs_ref.at[pl.ds(dst, 1), :], sem)

    def start(t, _):
        for r in range(TOP_K):
            copy(t, r).start()
        return 0

    def wait(t, _):
        for r in range(TOP_K):
            copy(t, r).wait()
        return 0

    lax.fori_loop(0, tm, start, 0)
    lax.fori_loop(0, tm, wait, 0)


def _dispatch(h1, pos_flat, tail_row, n_used, n_rows, tm, tr):
    N = h1.shape[0]
    kern = functools.partial(_dispatch_kernel, tm=tm, tr=tr, n_tiles=n_rows // tr)
    return pl.pallas_call(
        kern,
        grid_spec=pltpu.PrefetchScalarGridSpec(
            num_scalar_prefetch=3,
            grid=(N // tm,),
            in_specs=[pl.BlockSpec((tm, D_MODEL), lambda i, pos, tail, nused: (i, 0))],
            out_specs=pl.BlockSpec(memory_space=pl.ANY),
            scratch_shapes=[pltpu.VMEM((tr, D_MODEL), F32), pltpu.SemaphoreType.DMA(())]),
        out_shape=jax.ShapeDtypeStruct((n_rows, D_MODEL), F32),
        compiler_params=_cparams(("arbitrary",)),
        name="dispatch",
    )(pos_flat, tail_row, n_used, h1)


def _expert_kernel(te_ref, nused_ref, xs_ref, wg_ref, wu_ref, wd_ref, bg_ref, bu_ref, bd_ref, ys_ref,
                   wgb, wub, wdb):
    i = pl.program_id(0)
    used = i < nused_ref[0]
    fresh = jnp.logical_or(i == 0, te_ref[i] != te_ref[jnp.maximum(i - 1, 0)])

    @pl.when(jnp.logical_and(used, fresh))
    def _():
        wgb[...] = wg_ref[0].astype(BF16)
        wub[...] = wu_ref[0].astype(BF16)
        wdb[...] = wd_ref[0].astype(BF16)

    @pl.when(used)
    def _():
        x = xs_ref[...].astype(BF16)
        g = jnp.minimum(jnp.dot(x, wgb[...], preferred_element_type=F32) + bg_ref[0], SWIGLU_LIMIT)
        u = jnp.clip(jnp.dot(x, wub[...], preferred_element_type=F32) + bu_ref[0], -SWIGLU_LIMIT, SWIGLU_LIMIT)
        act = (u + 1.0) * g * _sigmoid(SWIGLU_ALPHA * g)
        ys_ref[...] = jnp.dot(act.astype(BF16), wdb[...], preferred_element_type=F32) + bd_ref[0]

    @pl.when(jnp.logical_not(used))
    def _():
        ys_ref[...] = jnp.zeros_like(ys_ref)


def _experts(xs, tile_expert, n_used, w_gate, b_gate, w_up, b_up, w_down, b_down, tr):
    P = xs.shape[0]
    wmap = lambda i, te, nu: (te[i], 0, 0)
    return pl.pallas_call(
        _expert_kernel,
        grid_spec=pltpu.PrefetchScalarGridSpec(
            num_scalar_prefetch=2,
            grid=(P // tr,),
            in_specs=[pl.BlockSpec((tr, D_MODEL), lambda i, te, nu: (i, 0)),
                      pl.BlockSpec((1, D_MODEL, D_FF), wmap), pl.BlockSpec((1, D_MODEL, D_FF), wmap),
                      pl.BlockSpec((1, D_FF, D_MODEL), wmap),
                      pl.BlockSpec((1, 1, D_FF), wmap), pl.BlockSpec((1, 1, D_FF), wmap),
                      pl.BlockSpec((1, 1, D_MODEL), wmap)],
            out_specs=pl.BlockSpec((tr, D_MODEL), lambda i, te, nu: (i, 0)),
            scratch_shapes=[pltpu.VMEM((D_MODEL, D_FF), BF16), pltpu.VMEM((D_MODEL, D_FF), BF16),
                            pltpu.VMEM((D_FF, D_MODEL), BF16)]),
        out_shape=jax.ShapeDtypeStruct((P, D_MODEL), F32),
        compiler_params=_cparams(("arbitrary",)),
        name="experts",
    )(tile_expert, n_used, xs, w_gate, w_up, w_down, b_gate[:, None, :], b_up[:, None, :], b_down[:, None, :])


def _combine_kernel(pos_ref, h1_ref, gate_ref, g_ref, b_ref, ys_ref, out_ref, buf, sems, *, tm, alpha):
    i = pl.program_id(0)
    slot = i % 2

    def copy(step, s, t, r):
        src = pos_ref[(step * tm + t) * TOP_K + r]
        return pltpu.make_async_copy(ys_ref.at[pl.ds(src, 1), :], buf.at[s, r, pl.ds(t, 1), :], sems.at[s])

    def start_tile(step, s):
        def start(t, _):
            for r in range(TOP_K):
                copy(step, s, t, r).start()
            return 0

        lax.fori_loop(0, tm, start, 0)

    def wait_tile(step, s):
        def wait(t, _):
            for r in range(TOP_K):
                copy(step, s, t, r).wait()
            return 0

        lax.fori_loop(0, tm, wait, 0)

    @pl.when(i == 0)
    def _():
        start_tile(0, 0)

    @pl.when(i + 1 < pl.num_programs(0))
    def _():
        start_tile(i + 1, 1 - slot)

    wait_tile(i, slot)
    gates = gate_ref[...]
    moe = gates[:, 0:1] * buf[slot, 0]
    for r in range(1, TOP_K):
        moe = moe + gates[:, r:r + 1] * buf[slot, r]
    out_ref[...] = _layer_norm(alpha * h1_ref[...] + moe, g_ref[...], b_ref[...])


def _combine(h1, gates, ys, pos_flat, ln_g, ln_b, tm, alpha):
    N = h1.shape[0]
    kern = functools.partial(_combine_kernel, tm=tm, alpha=alpha)
    return pl.pallas_call(
        kern,
        grid_spec=pltpu.PrefetchScalarGridSpec(
            num_scalar_prefetch=1,
            grid=(N // tm,),
            in_specs=[pl.BlockSpec((tm, D_MODEL), lambda i, pos: (i, 0)),
                      pl.BlockSpec((tm, LANES), lambda i, pos: (i, 0)),
                      pl.BlockSpec((1, D_MODEL), lambda i, pos: (0, 0)),
                      pl.BlockSpec((1, D_MODEL), lambda i, pos: (0, 0)),
                      pl.BlockSpec(memory_space=pl.ANY)],
            out_specs=pl.BlockSpec((tm, D_MODEL), lambda i, pos: (i, 0)),
            scratch_shapes=[pltpu.VMEM((2, TOP_K, tm, D_MODEL), F32), pltpu.SemaphoreType.DMA((2,))]),
        out_shape=jax.ShapeDtypeStruct((N, D_MODEL), F32),
        compiler_params=_cparams(("arbitrary",)),
        name="combine",
    )(pos_flat, h1, gates, ln_g[None, :], ln_b[None, :], ys)


def _pick(n, prefs):
    for p in prefs:
        if n % p == 0:
            return p
    raise ValueError(f"no tile size in {prefs} divides {n}")


def _moe_layout(idx, rank, counts, tr):
    tiles = (counts + tr - 1) // tr
    tile_end = jnp.cumsum(tiles)
    offs = (tile_end - tiles) * tr
    hot = idx[:, :, None] == jnp.arange(N_EXPERTS, dtype=I32)[None, None, :]
    pos = jnp.sum(jnp.where(hot, offs[None, None, :], 0), axis=-1) + rank
    n_tiles = (idx.shape[0] * TOP_K) // tr + N_EXPERTS
    tile_id = jnp.arange(n_tiles, dtype=I32)
    tile_expert = jnp.sum((tile_end[None, :] <= tile_id[:, None]).astype(I32), axis=1)
    tile_expert = jnp.minimum(tile_expert, N_EXPERTS - 1).astype(I32)
    tail_row = jnp.where(tiles > 0, (tile_end - 1) * tr, -1).astype(I32)
    return pos.reshape(-1).astype(I32), tile_expert, tile_end[-1:].astype(I32), tail_row, n_tiles * tr


def kernel(x, ln_in_g, ln_in_b, w_in, b_in, conv_w, conv_b, m_norm_g, w_out, ln1_g, ln1_b, w_router, b_router,
           w_gate, b_gate, w_up, b_up, w_down, b_down, ln2_g, ln2_b):
    B, T, D = x.shape
    depth = w_in.shape[0]
    assert D == D_MODEL and T % 256 == 0 and depth == 1
    alpha = (2.0 * depth) ** 0.25
    N = B * T
    tm_proj = _pick(T, (512, 256))
    L = 256
    TQ_SEL = 256
    TQ_ATT, TK_ATT = 512, _pick(T, (1024, 512, 256))
    tm_out = _pick(N, (512, 256))
    tm_disp = _pick(N, (512, 256))
    tm_comb = _pick(N, (256,))
    tr = 256

    h = x.reshape(N, D)
    for l in range(depth):
        wp, bp = _pack_in_weights(w_in[l], b_in[l])
        (h0, mq, mk, mv, mo_sig, aq, ak, av, xq, gm_sig, ga_sig, xk, small, small_t) = _in_proj(
            h, ln_in_g, ln_in_b, wp, bp, T, tm_proj)
        ym = _mlstm(mq, mk, mv, mo_sig, gm_sig, small, small_t, conv_w[l], conv_b[l], m_norm_g[l], B, T, L)
        mask = _select(xq, xk, small_t, B, T, TQ_SEL)
        ya = _attn(aq, ak, av, mask, ga_sig, B, T, TQ_ATT, TK_ATT)
        h1, gates, idx, rank, counts = _out_proj(ym, ya, h0, w_out[l].astype(BF16), ln1_g[l], ln1_b[l],
                                                 w_router[l], b_router[l], tm_out, alpha)
        pos, tile_expert, n_used, tail_row, n_rows = _moe_layout(idx[:, :TOP_K], rank[:, :TOP_K],
                                                                 counts[0, :N_EXPERTS].astype(I32), tr)
        xs = _dispatch(h1, pos, tail_row, n_used, n_rows, tm_disp, tr)
        ys = _experts(xs, tile_expert, n_used, w_gate[l], b_gate[l], w_up[l], b_up[l], w_down[l], b_down[l], tr)
        h = _combine(h1, gates, ys, pos, ln2_g[l], ln2_b[l], tm_comb, alpha)
    return h.reshape(B, T, D)
```

```python
import functools
import math

import jax
import jax.numpy as jnp
from jax import lax
from jax.experimental import pallas as pl
from jax.experimental.pallas import tpu as pltpu

F32 = jnp.float32
BF16 = jnp.bfloat16
I32 = jnp.int32

D_MODEL = 1024
CHUNK = 64
M_HEADS = 4
M_HEAD_DIM = D_MODEL // M_HEADS
M_WIDTH = M_HEADS * M_HEAD_DIM
CONV_WIDTH = 4
A_HEADS = 8
A_HEAD_DIM = D_MODEL // A_HEADS
A_WIDTH = A_HEADS * A_HEAD_DIM
IDX_HEADS = 8
IDX_DIM = 64
IDX_WIDTH = IDX_HEADS * IDX_DIM
TOPK_MAX = 256
ROPE_THETA = 10000.0
N_EXPERTS = 32
TOP_K = 4
D_FF = D_MODEL
SWIGLU_LIMIT = 7.0
SWIGLU_ALPHA = 1.702
LN_EPS = 1e-5

LANES = 128
NEG_BIG = -1e30
F32_TINY = 1.1754943508222875e-38
LOG2_E = 1.4426950408889634
ROUTE_W = 8
DMA_UNROLL = 4
VMEM_LIMIT = 56 * 1024 * 1024

_BIG_GROUPS = ("m_q", "m_k", "m_v", "m_o", "a_q", "a_k", "a_v", "x_q", "g_m", "g_a")
_GROUP_WIDTH = dict(m_q=M_WIDTH, m_k=M_WIDTH, m_v=M_WIDTH, m_o=M_WIDTH, a_q=A_WIDTH, a_k=A_WIDTH,
                    a_v=A_WIDTH, x_q=IDX_WIDTH, g_m=D_MODEL, g_a=D_MODEL, x_k=LANES, small=LANES)
_GROUP_ORDER = _BIG_GROUPS + ("x_k", "small")
_GROUP_START = {}
_off = 0
for _g in _GROUP_ORDER:
    _GROUP_START[_g] = _off
    _off += _GROUP_WIDTH[_g]
PACKED_WIDTH = _off


def _cparams(sem, vmem=VMEM_LIMIT):
    return pltpu.CompilerParams(dimension_semantics=sem, vmem_limit_bytes=vmem)


def _layer_norm(x, g, b):
    mu = jnp.mean(x, axis=-1, keepdims=True)
    xc = x - mu
    var = jnp.mean(xc * xc, axis=-1, keepdims=True)
    return xc * lax.rsqrt(var + LN_EPS) * g + b


def _sigmoid(x):
    return 1.0 / (1.0 + jnp.exp(-x))


def _in_proj_kernel(x_ref, g_ref, b_ref, w_ref, bias_ref, cosa_ref, sina_ref, cosb_ref, sinlo_ref, sinhi_ref,
                    h0_ref, mq_ref, mk_ref, mv_ref, mo_ref, aq_ref, ak_ref, av_ref, xq_ref, gm_ref, ga_ref,
                    xk_ref, small_ref, small_t_ref):
    h0 = _layer_norm(x_ref[...], g_ref[...], b_ref[...])
    h0_ref[...] = h0
    hb = h0.astype(BF16)

    def proj(name, c0=0, width=None):
        start = _GROUP_START[name] + c0
        width = _GROUP_WIDTH[name] if width is None else width
        y = jnp.dot(hb, w_ref[:, start:start + width], preferred_element_type=F32)
        return y + bias_ref[:, start:start + width]

    def rope_full(y):
        return y * cosa_ref[...] + pltpu.roll(y, 64, axis=1) * sina_ref[...]

    def rope_half(y):
        return (y * cosb_ref[...] + pltpu.roll(y, 96, axis=1) * sinlo_ref[...]
                + pltpu.roll(y, 32, axis=1) * sinhi_ref[...])

    mq_ref[...] = proj("m_q").astype(BF16)
    mk_ref[...] = proj("m_k").astype(BF16)
    mv_ref[...] = proj("m_v").astype(BF16)
    mo_ref[...] = _sigmoid(proj("m_o")).astype(BF16)
    gm_ref[...] = _sigmoid(proj("g_m")).astype(BF16)
    ga_ref[...] = _sigmoid(proj("g_a")).astype(BF16)
    av_ref[...] = proj("a_v").astype(BF16)
    q_scale = A_HEAD_DIM ** -0.5 * LOG2_E
    yq, yk = proj("a_q"), proj("a_k")
    for h in range(A_HEADS):
        sl = slice(h * LANES, (h + 1) * LANES)
        aq_ref[:, sl] = (rope_full(yq[:, sl]) * q_scale).astype(BF16)
        ak_ref[:, sl] = rope_full(yk[:, sl]).astype(BF16)
    yx = proj("x_q")
    for c in range(IDX_WIDTH // LANES):
        sl = slice(c * LANES, (c + 1) * LANES)
        xq_ref[:, sl] = rope_half(yx[:, sl]).astype(BF16)
    tail = proj("x_k", 0, 2 * LANES)
    xk_ref[...] = rope_half(tail[:, 0:LANES]).astype(BF16)
    small = tail[:, LANES:2 * LANES]
    small_ref[...] = small
    small_t_ref[...] = small.T


def _rope_tables(T):
    pos = jnp.arange(T, dtype=F32)[:, None]
    half_a = A_HEAD_DIM // 2
    inv_a = ROPE_THETA ** (-jnp.arange(half_a, dtype=F32) / half_a)
    ang_a = pos * inv_a[None, :]
    cos_a, sin_a = jnp.cos(ang_a), jnp.sin(ang_a)
    cosa = jnp.concatenate([cos_a, cos_a], axis=1)
    sina = jnp.concatenate([-sin_a, sin_a], axis=1)
    half_b = IDX_DIM // 2
    inv_b = ROPE_THETA ** (-jnp.arange(half_b, dtype=F32) / half_b)
    ang_b = pos * inv_b[None, :]
    cos_b, sin_b = jnp.cos(ang_b), jnp.sin(ang_b)
    zero = jnp.zeros_like(sin_b)
    cosb = jnp.concatenate([cos_b, cos_b, cos_b, cos_b], axis=1)
    sinlo = jnp.concatenate([-sin_b, zero, -sin_b, zero], axis=1)
    sinhi = jnp.concatenate([zero, sin_b, zero, sin_b], axis=1)
    return cosa, sina, cosb, sinlo, sinhi


def _pack_in_weights(w_all, b_all, l):
    sizes = (M_WIDTH, M_WIDTH, M_WIDTH, M_WIDTH, M_HEADS, M_HEADS, A_WIDTH, A_WIDTH, A_WIDTH,
             IDX_WIDTH, IDX_DIM, IDX_HEADS, D_MODEL, D_MODEL)
    names = ("m_q", "m_k", "m_v", "m_o", "m_i", "m_f", "a_q", "a_k", "a_v", "x_q", "x_k", "x_w", "g_m", "g_a")
    parts, start = {}, 0
    for n, s in zip(names, sizes):
        parts[n] = (w_all[l, :, start:start + s].astype(BF16), b_all[l, start:start + s].astype(F32))
        start += s
    rows = w_all.shape[1]

    def pad(n_cols):
        return jnp.zeros((rows, n_cols), BF16), jnp.zeros((n_cols,), F32)

    order = [parts[n] for n in _BIG_GROUPS]
    order += [parts["x_k"], pad(LANES - IDX_DIM)]
    order += [parts["m_i"], parts["m_f"], parts["x_w"], pad(LANES - 2 * M_HEADS - IDX_HEADS)]
    wp = jnp.concatenate([o[0] for o in order], axis=1)
    bp = jnp.concatenate([o[1] for o in order], axis=0)[None, :]
    return wp, bp


def _in_proj(x2, ln_g, ln_b, wp, bp, T, tm):
    N = x2.shape[0]
    nt = T // tm
    tables = _rope_tables(T)
    row = lambda i: (i, 0)
    const = lambda i: (0, 0)
    tab = lambda i: (i % nt, 0)
    widths = [D_MODEL, M_WIDTH, M_WIDTH, M_WIDTH, M_WIDTH, A_WIDTH, A_WIDTH, A_WIDTH, IDX_WIDTH, D_MODEL, D_MODEL,
              LANES, LANES]
    dtypes = [F32] + [BF16] * 11 + [F32]
    return pl.pallas_call(
        _in_proj_kernel,
        grid=(N // tm,),
        in_specs=[pl.BlockSpec((tm, D_MODEL), row),
                  pl.BlockSpec((1, D_MODEL), const), pl.BlockSpec((1, D_MODEL), const),
                  pl.BlockSpec((D_MODEL, PACKED_WIDTH), const, pipeline_mode=pl.Buffered(1)),
                  pl.BlockSpec((1, PACKED_WIDTH), const)]
                 + [pl.BlockSpec((tm, LANES), tab)] * 5,
        out_specs=[pl.BlockSpec((tm, wd), row) for wd in widths] + [pl.BlockSpec((LANES, tm), lambda i: (0, i))],
        out_shape=[jax.ShapeDtypeStruct((N, wd), dt) for wd, dt in zip(widths, dtypes)]
                  + [jax.ShapeDtypeStruct((LANES, N), F32)],
        compiler_params=_cparams(("parallel",)),
        name="in_proj",
    )(x2, ln_g[None, :], ln_b[None, :], wp, bp, *tables)


def _log_sigmoid(x):
    return jnp.minimum(x, 0.0) - jnp.log(1.0 + jnp.exp(-jnp.abs(x)))


def _split3(x):
    a1 = x.astype(BF16)
    r1 = x - a1.astype(F32)
    a2 = r1.astype(BF16)
    a3 = (r1 - a2.astype(F32)).astype(BF16)
    return a1, a2, a3


def _mlstm_kernel(q_ref, k_ref, v_ref, o_ref, gm_ref, gcol_ref, grow_ref, cwq_ref, cwk_ref, cbq_ref, cbk_ref,
                  ng_ref, out_ref, qprev, kprev, shift_ref, c_state, n_state, m_state, *, L):
    c = pl.program_id(2)
    halo = 8
    r = lax.broadcasted_iota(I32, (L, L), 0)
    s = lax.broadcasted_iota(I32, (L, L), 1)
    causal = r >= s

    @pl.when(c == 0)
    def _():
        qprev[...] = jnp.zeros_like(qprev)
        kprev[...] = jnp.zeros_like(kprev)
        c_state[...] = jnp.zeros_like(c_state)
        n_state[...] = jnp.zeros_like(n_state)
        m_state[...] = jnp.zeros_like(m_state)
        for j in range(1, CONV_WIDTH):
            shift_ref[j - 1] = jnp.where(r - s == j, 1.0, 0.0).astype(BF16)
        shift_ref[CONV_WIDTH - 1] = jnp.where(causal, 1.0, 0.0).astype(BF16)
        shift_ref[CONV_WIDTH] = jnp.where(r <= s, 1.0, 0.0).astype(BF16)

    row8 = lax.broadcasted_iota(I32, (halo, M_HEAD_DIM), 0)

    def conv_silu(prev_ref, x_ref, w_ref, b_ref):
        xb = x_ref[...]
        x = xb.astype(F32)
        acc = b_ref[...] + x * w_ref[CONV_WIDTH - 1:CONV_WIDTH, :]
        prev = prev_ref[...]
        head = jnp.zeros((halo, M_HEAD_DIM), F32)
        for j in range(1, CONV_WIDTH):
            wj = w_ref[CONV_WIDTH - 1 - j:CONV_WIDTH - j, :]
            acc = acc + jnp.dot(shift_ref[j - 1], xb, preferred_element_type=F32) * wj
            head = head + jnp.where(row8 < j, pltpu.roll(prev, j, axis=0), 0.0) * wj
        acc = jnp.concatenate([acc[0:halo, :] + head, acc[halo:, :]], axis=0)
        prev_ref[...] = x[L - halo:L, :]
        return acc * _sigmoid(acc)

    qc = conv_silu(qprev, q_ref, cwq_ref, cbq_ref) * (M_HEAD_DIM ** -0.5)
    kc = conv_silu(kprev, k_ref, cwk_ref, cbk_ref)
    v = v_ref[...]

    hd = pl.program_id(1)
    gcol = gcol_ref[...]
    lane = lax.broadcasted_iota(I32, (L, LANES), 1)
    i_col = jnp.sum(jnp.where(lane == hd, gcol, 0.0), axis=1, keepdims=True)
    lf_col = _log_sigmoid(jnp.sum(jnp.where(lane == hd + M_HEADS, gcol, 0.0), axis=1, keepdims=True))
    i_row = grow_ref[pl.ds(hd, 1), :]
    lf_row = _log_sigmoid(grow_ref[pl.ds(hd + M_HEADS, 1), :])

    b_col = sum(jnp.dot(shift_ref[CONV_WIDTH - 1], part, preferred_element_type=F32)
                for part in _split3(jnp.broadcast_to(lf_col, (L, LANES))))[:, 0:1]
    b_row = sum(jnp.dot(part, shift_ref[CONV_WIDTH], preferred_element_type=F32)
                for part in _split3(jnp.broadcast_to(lf_row, (8, L))))[0:1, :]

    m_prev = m_state[...]
    dm = jnp.where(causal, b_col - b_row + i_row, -jnp.inf)
    m_inter = b_col + m_prev
    m_t = jnp.maximum(m_inter, jnp.max(dm, axis=1, keepdims=True))
    qb, kb = qc.astype(BF16), kc.astype(BF16)
    qk = lax.dot_general(qb, kb, (((1,), (1,)), ((), ())), preferred_element_type=F32)
    sm = qk * jnp.exp(dm - m_t)
    w_inter = jnp.exp(m_inter - m_t)
    cb = c_state[...].astype(BF16)
    num = (jnp.dot(sm.astype(BF16), v, preferred_element_type=F32)
           + w_inter * jnp.dot(qb, cb, preferred_element_type=F32))
    den = (jnp.sum(sm, axis=1, keepdims=True)
           + w_inter * jnp.sum(qc * n_state[...], axis=1, keepdims=True))
    hh = num / jnp.maximum(jnp.abs(den), jnp.exp(-m_t))

    b_last = b_col[L - 1:L, :]
    g_row = b_last - b_row + i_row
    m_new = jnp.maximum(b_last + m_prev, jnp.max(g_row, axis=1, keepdims=True))
    wg_col = jnp.exp(b_last - b_col + i_col - m_new)
    decay = jnp.exp(b_last + m_prev - m_new)
    kw = kc * wg_col
    kv = lax.dot_general(kw.astype(BF16), v, (((0,), (0,)), ((), ())), preferred_element_type=F32)
    c_state[...] = decay * c_state[...] + kv
    n_state[...] = decay * n_state[...] + jnp.sum(kw, axis=0, keepdims=True)
    m_state[...] = m_new

    mu = jnp.mean(hh, axis=1, keepdims=True)
    hc = hh - mu
    var = jnp.mean(hc * hc, axis=1, keepdims=True)
    hn = hc * lax.rsqrt(var + LN_EPS) * ng_ref[...]
    out_ref[...] = (hn * o_ref[...].astype(F32) * gm_ref[...].astype(F32)).astype(BF16)


def _mlstm(mq, mk, mv, mo_sig, gm_sig, small, small_t, conv_w, conv_b, norm_g, B, T, L):
    nL = T // L
    blk = lambda b, h, c: (b * nL + c, h)
    head = lambda b, h, c: (0, h)
    kern = functools.partial(_mlstm_kernel, L=L)
    return pl.pallas_call(
        kern,
        grid=(B, M_HEADS, nL),
        in_specs=[pl.BlockSpec((L, M_HEAD_DIM), blk)] * 5
                 + [pl.BlockSpec((L, LANES), lambda b, h, c: (b * nL + c, 0)),
                    pl.BlockSpec((2 * M_HEADS, L), lambda b, h, c: (0, b * nL + c)),
                    pl.BlockSpec((CONV_WIDTH, M_HEAD_DIM), head),
                    pl.BlockSpec((CONV_WIDTH, M_HEAD_DIM), head),
                    pl.BlockSpec((1, M_HEAD_DIM), head),
                    pl.BlockSpec((1, M_HEAD_DIM), head),
                    pl.BlockSpec((1, M_HEAD_DIM), head)],
        out_specs=pl.BlockSpec((L, M_HEAD_DIM), blk),
        out_shape=jax.ShapeDtypeStruct((B * T, M_WIDTH), BF16),
        scratch_shapes=[pltpu.VMEM((8, M_HEAD_DIM), F32), pltpu.VMEM((8, M_HEAD_DIM), F32),
                        pltpu.VMEM((CONV_WIDTH + 1, L, L), BF16),
                        pltpu.VMEM((M_HEAD_DIM, M_HEAD_DIM), F32), pltpu.VMEM((1, M_HEAD_DIM), F32),
                        pltpu.VMEM((1, 1), F32)],
        compiler_params=_cparams(("parallel", "parallel", "arbitrary")),
        name="mlstm",
    )(mq, mk, mv, mo_sig, gm_sig, small, small_t,
      conv_w[:, :M_WIDTH], conv_w[:, M_WIDTH:], conv_b[None, :M_WIDTH], conv_b[None, M_WIDTH:], norm_g[None, :])


def _select_kernel(xq_ref, xk_ref, xw_ref, mask_ref, sc_ref, ti_ref, jsel_ref, *raw_refs, TQ, T, top_k):
    qi = pl.program_id(1)
    nkt = qi + 1
    SUB = 8
    NACC = 4
    kf = float(top_k)
    w_all = xw_ref[...] * ((IDX_DIM ** -0.5) * (IDX_HEADS ** -0.5))

    key_chunk = lax.broadcasted_iota(I32, (TQ, TQ), 0) // CHUNK
    qry_chunk = lax.broadcasted_iota(I32, (TQ, TQ), 1) // CHUNK
    diag_ok = key_chunk <= qry_chunk

    def fold(op, acc, tile):
        for i in range(TQ // SUB):
            acc = op(acc, tile[i * SUB:(i + 1) * SUB, :])
        return acc

    def products(kt, raw_ref):
        koff = pl.multiple_of(jnp.minimum(kt, nkt - 1) * TQ, TQ)
        kblk = xk_ref[pl.ds(koff, TQ), 0:IDX_DIM]
        for h in range(IDX_HEADS):
            qh = xq_ref[:, h * IDX_DIM:(h + 1) * IDX_DIM]
            raw_ref[h] = lax.dot_general(kblk, qh, (((1,), (1,)), ((), ())), preferred_element_type=F32)

    def combine(kt, raw_ref, mx, mn):
        kt = jnp.minimum(kt, nkt - 1)
        koff = pl.multiple_of(kt * TQ, TQ)
        acc = jnp.zeros((TQ, TQ), F32)
        for h in range(IDX_HEADS):
            acc = acc + jnp.maximum(raw_ref[h], 0.0) * w_all[h:h + 1, :]
        ok = jnp.logical_or(kt < qi, diag_ok)
        sc_ref[pl.ds(koff, TQ), :] = jnp.where(ok, acc, NEG_BIG)
        mx = fold(jnp.maximum, mx, jnp.where(ok, acc, NEG_BIG))
        mn = fold(jnp.minimum, mn, jnp.where(ok, acc, -NEG_BIG))
        return mx, mn

    def score_pair(j, carry):
        mx, mn = carry
        products(2 * j + 1, raw_refs[1])
        mx, mn = combine(2 * j, raw_refs[0], mx, mn)
        products(2 * j + 2, raw_refs[0])
        return combine(2 * j + 1, raw_refs[1], mx, mn)

    products(0, raw_refs[0])
    mx, mn = lax.fori_loop(0, (nkt + 1) // 2, score_pair,
                           (jnp.full((SUB, TQ), NEG_BIG, F32), jnp.full((SUB, TQ), -NEG_BIG, F32)))
    rep = lambda v: jnp.broadcast_to(v, (SUB, TQ))
    mx = rep(jnp.max(mx, axis=0, keepdims=True))
    mn = rep(jnp.min(mn, axis=0, keepdims=True))

    def count(pred, ref=sc_ref, lanes=slice(0, TQ)):
        width = lanes.stop - lanes.start

        def body(kt, accs):
            koff = pl.multiple_of(kt * TQ, TQ)
            tile = ref[pl.ds(koff, TQ), lanes]
            accs = list(accs)
            for i in range(TQ // SUB):
                sv = tile[i * SUB:(i + 1) * SUB, :]
                accs[i % NACC] = accs[i % NACC] + jnp.where(pred(sv, koff + i * SUB), 1.0, 0.0)
            return tuple(accs)

        accs = lax.fori_loop(0, nkt, body, tuple(jnp.zeros((SUB, width), F32) for _ in range(NACC)))
        tot = accs[0]
        for a in accs[1:]:
            tot = tot + a
        return jnp.broadcast_to(jnp.sum(tot, axis=0, keepdims=True), (SUB, width))

    def any_query(flag):
        return jnp.max(jnp.where(flag, 1.0, 0.0)) > 0.0

    q_id = qi * TQ + lax.broadcasted_iota(I32, (SUB, TQ), 1)
    n_adm = ((q_id // CHUNK + 1) * CHUNK).astype(F32)
    lo0 = mn
    hi0 = mx + jnp.maximum(jnp.abs(mx), 1e-30) * 1e-3
    c_zero = count(lambda sv, off: sv >= 0.0)
    c_pos = count(lambda sv, off: sv >= F32_TINY)
    pos_side = jnp.logical_and(c_pos >= kf, lo0 < F32_TINY)
    zero_hit = jnp.logical_and(c_pos < kf, c_zero >= kf)
    neg_side = jnp.logical_and(c_zero < kf, hi0 > 0.0)
    lo1 = jnp.where(pos_side, F32_TINY, jnp.where(zero_hit, 0.0, lo0))
    clo1 = jnp.where(pos_side, c_pos, jnp.where(zero_hit, c_zero, n_adm))
    hi1 = jnp.where(zero_hit, F32_TINY, jnp.where(neg_side, 0.0, hi0))
    chi1 = jnp.where(zero_hit, c_pos, jnp.where(neg_side, c_zero, 0.0))

    def active(lo, hi, clo):
        mid = 0.5 * lo + 0.5 * hi
        return jnp.logical_and(clo > kf, jnp.logical_and(mid > lo, mid < hi)), mid

    def halve(lo, hi, clo, chi):
        act, mid = active(lo, hi, clo)
        cnt = count(lambda sv, off: sv >= mid)
        ge = cnt >= kf
        up = jnp.logical_and(act, ge)
        dn = jnp.logical_and(act, jnp.logical_not(ge))
        return (jnp.where(up, mid, lo), jnp.where(dn, mid, hi), jnp.where(up, cnt, clo), jnp.where(dn, cnt, chi))

    def cond(carry):
        return jnp.logical_and(carry[1], carry[0] < 200)

    def body(carry):
        it, _, lo, hi, clo, chi = carry
        lo, hi, clo, chi = halve(*halve(lo, hi, clo, chi))
        return it + 1, any_query(active(lo, hi, clo)[0]), lo, hi, clo, chi

    _, _, lo, hi, clo, chi = lax.while_loop(
        cond, body, (jnp.int32(0), any_query(active(lo1, hi1, clo1)[0]), lo1, hi1, clo1, chi1))

    def key_index(off, width=TQ):
        return (off + lax.broadcasted_iota(I32, (SUB, width), 0)).astype(F32)

    def write_mask(keep):
        def write(kt, _):
            koff = pl.multiple_of(kt * TQ, TQ)
            tile = sc_ref[pl.ds(koff, TQ), :]
            slabs = []
            for i in range(TQ // SUB):
                sv = tile[i * SUB:(i + 1) * SUB, :]
                slabs.append(jnp.where(keep(sv, koff + i * SUB), 1, 0))
            mask_ref[:, pl.ds(koff, TQ)] = jnp.concatenate(slabs, axis=0).T.astype(jnp.int8)
            return 0

        lax.fori_loop(0, nkt, write, 0)

    def resolve_ties(lanes):
        width = lanes.stop - lanes.start
        lo_g, hi_g, clo_g, chi_g = lo[:, lanes], hi[:, lanes], clo[:, lanes], chi[:, lanes]
        tie = clo_g > kf

        @pl.when(any_query(tie))
        def _():
            need = kf - chi_g
            n_tie = clo_g - chi_g

            def stage(kt, _):
                koff = pl.multiple_of(kt * TQ, TQ)
                tile = sc_ref[pl.ds(koff, TQ), lanes]
                slabs = []
                for i in range(TQ // SUB):
                    sv = tile[i * SUB:(i + 1) * SUB, :]
                    slabs.append(jnp.where(jnp.logical_and(sv >= lo_g, sv < hi_g),
                                           key_index(koff + i * SUB, width), 2.0 * T))
                ti_ref[pl.ds(koff, TQ), lanes] = jnp.concatenate(slabs, axis=0)
                return 0

            lax.fori_loop(0, nkt, stage, 0)

            def jactive(jlo, jhi):
                return jnp.logical_and(tie, jhi - jlo > 1.5)

            def jcond(carry):
                return jnp.logical_and(carry[1], carry[0] < 64)

            def jbody(carry):
                it, _, jlo, jhi, cjlo, cjhi = carry
                act = jactive(jlo, jhi)
                frac = (need - cjlo) / jnp.maximum(cjhi - cjlo, 1.0)
                jint = jnp.ceil(jlo + (jhi - jlo) * frac)
                jmid = jnp.floor(0.5 * (jlo + jhi))
                even = jnp.full((SUB, width), (it % 2 == 0).astype(F32), F32) > 0.5
                j = jnp.where(even, jnp.clip(jint, jlo + 1.0, jhi - 1.0), jmid)
                cnt = count(lambda tv, off: tv < j, ti_ref, lanes)
                ge = cnt >= need
                up = jnp.logical_and(act, ge)
                dn = jnp.logical_and(act, jnp.logical_not(ge))
                jhi = jnp.where(up, j, jhi)
                cjhi = jnp.where(up, cnt, cjhi)
                jlo = jnp.where(dn, j, jlo)
                cjlo = jnp.where(dn, cnt, cjlo)
                return it + 1, any_query(jactive(jlo, jhi)), jlo, jhi, cjlo, cjhi

            jlo0 = jnp.zeros((SUB, width), F32)
            jhi0 = jnp.full((SUB, width), float(T), F32)
            _, _, _, jhi, _, _ = lax.while_loop(
                jcond, jbody, (jnp.int32(0), any_query(jactive(jlo0, jhi0)), jlo0, jhi0, jlo0, n_tie))
            jsel_ref[:, lanes] = jnp.where(tie, jhi, float(T))

    has_tie = any_query(clo > kf)

    @pl.when(jnp.logical_not(has_tie))
    def _():
        write_mask(lambda sv, off: sv >= lo)

    @pl.when(has_tie)
    def _():
        jsel_ref[...] = jnp.full((SUB, TQ), float(T), F32)
        for g in range(TQ // LANES):
            resolve_ties(slice(g * LANES, (g + 1) * LANES))
        jsel = jsel_ref[...]
        write_mask(lambda sv, off: jnp.logical_and(
            sv >= lo, jnp.logical_or(sv >= hi, key_index(off) < jsel)))

    def zero_tile(kt, _):
        mask_ref[:, pl.ds(pl.multiple_of(kt * TQ, TQ), TQ)] = jnp.zeros((TQ, TQ), jnp.int8)
        return 0

    lax.fori_loop(nkt, T // TQ, zero_tile, 0)


def _select(xq, xk, small_t, B, T, TQ):
    nq = T // TQ
    top_k = min(TOPK_MAX, T // 4)
    assert 2 * M_HEADS == IDX_HEADS
    kern = functools.partial(_select_kernel, TQ=TQ, T=T, top_k=top_k)
    return pl.pallas_call(
        kern,
        grid=(B, nq),
        in_specs=[pl.BlockSpec((TQ, IDX_WIDTH), lambda b, q: (b * nq + q, 0)),
                  pl.BlockSpec((T, LANES), lambda b, q: (b, 0)),
                  pl.BlockSpec((IDX_HEADS, TQ), lambda b, q: (1, b * nq + q))],
        out_specs=pl.BlockSpec((TQ, T), lambda b, q: (b * nq + q, 0)),
        out_shape=jax.ShapeDtypeStruct((B * T, T), jnp.int8),
        scratch_shapes=[pltpu.VMEM((T, TQ), F32), pltpu.VMEM((T, TQ), F32), pltpu.VMEM((8, TQ), F32),
                        pltpu.VMEM((IDX_HEADS, TQ, TQ), F32), pltpu.VMEM((IDX_HEADS, TQ, TQ), F32)],
        compiler_params=_cparams(("parallel", "parallel")),
        name="select",
    )(xq, xk, small_t)


def _attn_kernel(qtab_ref, ktab_ref, q_ref, k_ref, v_ref, mask_ref, g_ref, out_ref, bias_ref, *head_refs, TQ, TK):
    vext_refs = head_refs[0:A_HEADS]
    m_refs = head_refs[A_HEADS:2 * A_HEADS]
    acc_refs = head_refs[2 * A_HEADS:3 * A_HEADS]
    step = pl.program_id(1)
    qi = qtab_ref[step]
    kt = ktab_ref[step]
    last = ((qi + 1) * TQ - 1) // TK

    @pl.when(kt == 0)
    def _():
        for h in range(A_HEADS):
            m_refs[h][...] = jnp.full((TQ, LANES), NEG_BIG, F32)
            acc_refs[h][...] = jnp.zeros((TQ, 2 * A_HEAD_DIM), F32)
            vext_refs[h][:, A_HEAD_DIM:] = jnp.ones((TK, A_HEAD_DIM), BF16)

    bias_ref[...] = ((mask_ref[...].astype(F32) - 1.0) * (-NEG_BIG)).astype(BF16)
    for h in range(A_HEADS):
        sl = slice(h * A_HEAD_DIM, (h + 1) * A_HEAD_DIM)
        vext_refs[h][:, 0:A_HEAD_DIM] = v_ref[:, sl]
        s = lax.dot_general(q_ref[:, sl], k_ref[:, sl], (((1,), (1,)), ((), ())), preferred_element_type=F32)
        sb = s.astype(BF16) + bias_ref[...]
        m_old = m_refs[h][...]
        mx = jnp.max(sb, axis=1, keepdims=True).astype(F32)
        m_new = jnp.maximum(m_old, jnp.broadcast_to(mx, (TQ, LANES)))
        alpha = jnp.exp2(m_old - m_new)
        p = jnp.exp2(sb - m_new[:, 0:1].astype(BF16))
        pv = jnp.dot(p, vext_refs[h][...], preferred_element_type=F32)
        acc_refs[h][...] = jnp.concatenate([alpha, alpha], axis=1) * acc_refs[h][...] + pv
        m_refs[h][...] = m_new

    @pl.when(kt == last)
    def _():
        for h in range(A_HEADS):
            sl = slice(h * A_HEAD_DIM, (h + 1) * A_HEAD_DIM)
            num = acc_refs[h][:, 0:A_HEAD_DIM]
            den = acc_refs[h][:, A_HEAD_DIM:]
            out_ref[:, sl] = (num / den * g_ref[:, sl].astype(F32)).astype(BF16)


def _attn(aq, ak, av, mask, ga_sig, B, T, TQ, TK):
    nq, nk = T // TQ, T // TK
    pairs = [(q, k) for q in range(nq) for k in range(((q + 1) * TQ - 1) // TK + 1)]
    qtab = jnp.asarray([p[0] for p in pairs], I32)
    ktab = jnp.asarray([p[1] for p in pairs], I32)

    qmap = lambda b, s, qt, kt: (b * nq + qt[s], 0)
    kv_map = lambda b, s, qt, kt: (b * nk + kt[s], 0)
    mask_map = lambda b, s, qt, kt: (b * nq + qt[s], kt[s])
    kern = functools.partial(_attn_kernel, TQ=TQ, TK=TK)
    return pl.pallas_call(
        kern,
        grid_spec=pltpu.PrefetchScalarGridSpec(
            num_scalar_prefetch=2,
            grid=(B, len(pairs)),
            in_specs=[pl.BlockSpec((TQ, A_WIDTH), qmap),
                      pl.BlockSpec((TK, A_WIDTH), kv_map),
                      pl.BlockSpec((TK, A_WIDTH), kv_map),
                      pl.BlockSpec((TQ, TK), mask_map),
                      pl.BlockSpec((TQ, A_WIDTH), qmap)],
            out_specs=pl.BlockSpec((TQ, A_WIDTH), qmap),
            scratch_shapes=[pltpu.VMEM((TQ, TK), BF16)]
                           + [pltpu.VMEM((TK, 2 * A_HEAD_DIM), BF16)] * A_HEADS
                           + [pltpu.VMEM((TQ, LANES), F32)] * A_HEADS
                           + [pltpu.VMEM((TQ, 2 * A_HEAD_DIM), F32)] * A_HEADS),
        out_shape=jax.ShapeDtypeStruct((B * T, A_WIDTH), BF16),
        compiler_params=_cparams(("parallel", "arbitrary")),
        name="attn",
    )(qtab, ktab, aq, ak, av, mask, ga_sig)


def _out_proj_kernel(ym_ref, ya_ref, h0_ref, wo_ref, g_ref, b_ref, wr_hi_ref, wr_lo_ref, br_ref,
                     h1_ref, gate_ref, idx_ref, rank_ref, cnt_ref, carry_ref, *, tm, alpha):
    i = pl.program_id(0)

    @pl.when(i == 0)
    def _():
        carry_ref[...] = jnp.zeros_like(carry_ref)

    merged = (ym_ref[...].astype(F32) + ya_ref[...].astype(F32)).astype(BF16)
    y = jnp.dot(merged, wo_ref[...], preferred_element_type=F32)
    h1 = _layer_norm(alpha * h0_ref[...] + y, g_ref[...], b_ref[...])
    h1_ref[...] = h1

    h_hi = h1.astype(BF16)
    h_lo = (h1 - h_hi.astype(F32)).astype(BF16)
    logits = (jnp.dot(h_hi, wr_hi_ref[...], preferred_element_type=F32)
              + jnp.dot(h_hi, wr_lo_ref[...], preferred_element_type=F32)
              + jnp.dot(h_lo, wr_hi_ref[...], preferred_element_type=F32)) + br_ref[...]
    lane = lax.broadcasted_iota(I32, (tm, LANES), 1)
    vals, hots = [], []
    idx_out = jnp.zeros((tm, LANES), I32)
    work = logits
    for r in range(TOP_K):
        mx = jnp.max(work, axis=1, keepdims=True)
        first = jnp.min(jnp.where(work == mx, lane, LANES), axis=1, keepdims=True)
        hot = lane == first
        vals.append(mx)
        hots.append(hot)
        idx_out = jnp.where(lane == r, first, idx_out)
        work = jnp.where(hot, -jnp.inf, work)
    exps = [jnp.exp(v - vals[0]) for v in vals]
    tot = exps[0] + exps[1] + exps[2] + exps[3]
    gate_out = jnp.zeros((tm, LANES), F32)
    chosen = jnp.zeros((tm, LANES), F32)
    for r in range(TOP_K):
        gate_out = jnp.where(lane == r, exps[r] / tot, gate_out)
        chosen = chosen + jnp.where(hots[r], 1.0, 0.0)

    rr = lax.broadcasted_iota(I32, (tm, tm), 0)
    cc = lax.broadcasted_iota(I32, (tm, tm), 1)
    strict = (rr > cc).astype(BF16)
    before = jnp.dot(strict, chosen.astype(BF16), preferred_element_type=F32) + carry_ref[...]
    rank_out = jnp.zeros((tm, LANES), I32)
    for r in range(TOP_K):
        rk = jnp.sum(jnp.where(hots[r], before, 0.0), axis=1, keepdims=True)
        rank_out = jnp.where(lane == r, rk.astype(I32), rank_out)
    carry_ref[...] = carry_ref[...] + jnp.sum(chosen, axis=0, keepdims=True)

    gate_ref[...] = gate_out[:, 0:ROUTE_W]
    idx_ref[...] = idx_out[:, 0:ROUTE_W]
    rank_ref[...] = rank_out[:, 0:ROUTE_W]
    cnt_ref[...] = carry_ref[...]


def _out_proj(ym, ya, h0, w_out_b, ln_g, ln_b, w_router, b_router, tm, alpha):
    N = ym.shape[0]
    wr = jnp.zeros((D_MODEL, LANES), F32).at[:, :N_EXPERTS].set(w_router)
    wr_hi = wr.astype(BF16)
    wr_lo = (wr - wr_hi.astype(F32)).astype(BF16)
    br = jnp.full((1, LANES), NEG_BIG, F32).at[0, :N_EXPERTS].set(b_router)
    row = lambda i: (i, 0)
    const = lambda i: (0, 0)
    kern = functools.partial(_out_proj_kernel, tm=tm, alpha=alpha)
    return pl.pallas_call(
        kern,
        grid=(N // tm,),
        in_specs=[pl.BlockSpec((tm, D_MODEL), row), pl.BlockSpec((tm, D_MODEL), row),
                  pl.BlockSpec((tm, D_MODEL), row),
                  pl.BlockSpec((D_MODEL, D_MODEL), const),
                  pl.BlockSpec((1, D_MODEL), const), pl.BlockSpec((1, D_MODEL), const),
                  pl.BlockSpec((D_MODEL, LANES), const), pl.BlockSpec((D_MODEL, LANES), const),
                  pl.BlockSpec((1, LANES), const)],
        out_specs=[pl.BlockSpec((tm, D_MODEL), row), pl.BlockSpec((tm, ROUTE_W), row),
                   pl.BlockSpec((tm, ROUTE_W), row), pl.BlockSpec((tm, ROUTE_W), row),
                   pl.BlockSpec((1, LANES), const)],
        out_shape=[jax.ShapeDtypeStruct((N, D_MODEL), F32), jax.ShapeDtypeStruct((N, ROUTE_W), F32),
                   jax.ShapeDtypeStruct((N, ROUTE_W), I32), jax.ShapeDtypeStruct((N, ROUTE_W), I32),
                   jax.ShapeDtypeStruct((1, LANES), F32)],
        scratch_shapes=[pltpu.VMEM((1, LANES), F32)],
        compiler_params=_cparams(("arbitrary",)),
        name="out_proj",
    )(ym, ya, h0, w_out_b, ln_g[None, :], ln_b[None, :], wr_hi, wr_lo, br)


def _dispatch_kernel(pos_ref, tail_ref, nused_ref, h_ref, xs_ref, zero_ref, sem, *, tm, tr, n_tiles):
    base = pl.program_id(0) * (tm * TOP_K)

    @pl.when(pl.program_id(0) == 0)
    def _():
        zero_ref[...] = jnp.zeros_like(zero_ref)

        def clear(row0):
            return pltpu.make_async_copy(zero_ref, xs_ref.at[pl.ds(pl.multiple_of(row0, tr), tr), :], sem)

        def start_tail(e, _):
            @pl.when(tail_ref[e] >= 0)
            def _():
                clear(jnp.maximum(tail_ref[e], 0)).start()
            return 0

        def wait_tail(e, _):
            @pl.when(tail_ref[e] >= 0)
            def _():
                clear(jnp.maximum(tail_ref[e], 0)).wait()
            return 0

        def start_unused(t, _):
            clear(t * tr).start()
            return 0

        def wait_unused(t, _):
            clear(t * tr).wait()
            return 0

        lax.fori_loop(0, N_EXPERTS, start_tail, 0)
        lax.fori_loop(nused_ref[0], n_tiles, start_unused, 0)
        lax.fori_loop(0, N_EXPERTS, wait_tail, 0)
        lax.fori_loop(nused_ref[0], n_tiles, wait_unused, 0)

    def copy(t, r):
        dst = pos_ref[base + t * TOP_K + r]
        return pltpu.make_async_copy(h_ref.at[pl.ds(t, 1), :], xs_ref.at[pl.ds(dst, 1), :], sem)

    def start(t, _):
        for r in range(TOP_K):
            copy(t, r).start()
        return 0

    def wait(t, _):
        for r in range(TOP_K):
            copy(t, r).wait()
        return 0

    lax.fori_loop(0, tm, start, 0, unroll=DMA_UNROLL)
    lax.fori_loop(0, tm, wait, 0, unroll=DMA_UNROLL)


def _dispatch(h1, pos_flat, tail_row, n_used, n_rows, tm, tr):
    N = h1.shape[0]
    kern = functools.partial(_dispatch_kernel, tm=tm, tr=tr, n_tiles=n_rows // tr)
    return pl.pallas_call(
        kern,
        grid_spec=pltpu.PrefetchScalarGridSpec(
            num_scalar_prefetch=3,
            grid=(N // tm,),
            in_specs=[pl.BlockSpec((tm, D_MODEL), lambda i, pos, tail, nused: (i, 0))],
            out_specs=pl.BlockSpec(memory_space=pl.ANY),
            scratch_shapes=[pltpu.VMEM((tr, D_MODEL), F32), pltpu.SemaphoreType.DMA(())]),
        out_shape=jax.ShapeDtypeStruct((n_rows, D_MODEL), F32),
        compiler_params=_cparams(("arbitrary",)),
        name="dispatch",
    )(pos_flat, tail_row, n_used, h1)


def _expert_kernel(te_ref, nused_ref, xs_ref, wg_ref, wu_ref, wd_ref, bg_ref, bu_ref, bd_ref, ys_ref,
                   wgb, wub, wdb):
    i = pl.program_id(0)
    used = i < nused_ref[0]
    fresh = jnp.logical_or(i == 0, te_ref[i] != te_ref[jnp.maximum(i - 1, 0)])

    @pl.when(jnp.logical_and(used, fresh))
    def _():
        wgb[...] = wg_ref[0].astype(BF16)
        wub[...] = wu_ref[0].astype(BF16)
        wdb[...] = wd_ref[0].astype(BF16)

    @pl.when(used)
    def _():
        x = xs_ref[...].astype(BF16)
        g = jnp.minimum(jnp.dot(x, wgb[...], preferred_element_type=F32) + bg_ref[0], SWIGLU_LIMIT)
        u = jnp.clip(jnp.dot(x, wub[...], preferred_element_type=F32) + bu_ref[0], -SWIGLU_LIMIT, SWIGLU_LIMIT)
        act = (u + 1.0) * g * _sigmoid(SWIGLU_ALPHA * g)
        ys_ref[...] = jnp.dot(act.astype(BF16), wdb[...], preferred_element_type=F32) + bd_ref[0]

    @pl.when(jnp.logical_not(used))
    def _():
        ys_ref[...] = jnp.zeros_like(ys_ref)


def _experts(xs, tile_expert, n_used, w_gate, b_gate, w_up, b_up, w_down, b_down, tr):
    P = xs.shape[0]
    wmap = lambda i, te, nu: (te[i], 0, 0)
    return pl.pallas_call(
        _expert_kernel,
        grid_spec=pltpu.PrefetchScalarGridSpec(
            num_scalar_prefetch=2,
            grid=(P // tr,),
            in_specs=[pl.BlockSpec((tr, D_MODEL), lambda i, te, nu: (i, 0)),
                      pl.BlockSpec((1, D_MODEL, D_FF), wmap), pl.BlockSpec((1, D_MODEL, D_FF), wmap),
                      pl.BlockSpec((1, D_FF, D_MODEL), wmap),
                      pl.BlockSpec((1, 1, D_FF), wmap), pl.BlockSpec((1, 1, D_FF), wmap),
                      pl.BlockSpec((1, 1, D_MODEL), wmap)],
            out_specs=pl.BlockSpec((tr, D_MODEL), lambda i, te, nu: (i, 0)),
            scratch_shapes=[pltpu.VMEM((D_MODEL, D_FF), BF16), pltpu.VMEM((D_MODEL, D_FF), BF16),
                            pltpu.VMEM((D_FF, D_MODEL), BF16)]),
        out_shape=jax.ShapeDtypeStruct((P, D_MODEL), F32),
        compiler_params=_cparams(("arbitrary",)),
        name="experts",
    )(tile_expert, n_used, xs, w_gate, w_up, w_down, b_gate[:, None, :], b_up[:, None, :], b_down[:, None, :])


def _combine_kernel(pos_ref, h1_ref, gate_ref, g_ref, b_ref, ys_ref, out_ref, buf, sems, *, tm, alpha):
    i = pl.program_id(0)
    slot = i % 2

    def copy(step, s, t, r):
        src = pos_ref[(step * tm + t) * TOP_K + r]
        return pltpu.make_async_copy(ys_ref.at[pl.ds(src, 1), :], buf.at[s, r, pl.ds(t, 1), :], sems.at[s])

    def start_tile(step, s):
        def start(t, _):
            for r in range(TOP_K):
                copy(step, s, t, r).start()
            return 0

        lax.fori_loop(0, tm, start, 0, unroll=DMA_UNROLL)

    def wait_tile(step, s):
        def wait(t, _):
            for r in range(TOP_K):
                copy(step, s, t, r).wait()
            return 0

        lax.fori_loop(0, tm, wait, 0, unroll=DMA_UNROLL)

    @pl.when(i == 0)
    def _():
        start_tile(0, 0)

    @pl.when(i + 1 < pl.num_programs(0))
    def _():
        start_tile(i + 1, 1 - slot)

    wait_tile(i, slot)
    gates = gate_ref[...]
    moe = gates[:, 0:1] * buf[slot, 0]
    for r in range(1, TOP_K):
        moe = moe + gates[:, r:r + 1] * buf[slot, r]
    out_ref[...] = _layer_norm(alpha * h1_ref[...] + moe, g_ref[...], b_ref[...])


def _combine(h1, gates, ys, pos_flat, ln_g, ln_b, tm, alpha):
    N = h1.shape[0]
    kern = functools.partial(_combine_kernel, tm=tm, alpha=alpha)
    return pl.pallas_call(
        kern,
        grid_spec=pltpu.PrefetchScalarGridSpec(
            num_scalar_prefetch=1,
            grid=(N // tm,),
            in_specs=[pl.BlockSpec((tm, D_MODEL), lambda i, pos: (i, 0)),
                      pl.BlockSpec((tm, ROUTE_W), lambda i, pos: (i, 0)),
                      pl.BlockSpec((1, D_MODEL), lambda i, pos: (0, 0)),
                      pl.BlockSpec((1, D_MODEL), lambda i, pos: (0, 0)),
                      pl.BlockSpec(memory_space=pl.ANY)],
            out_specs=pl.BlockSpec((tm, D_MODEL), lambda i, pos: (i, 0)),
            scratch_shapes=[pltpu.VMEM((2, TOP_K, tm, D_MODEL), F32), pltpu.SemaphoreType.DMA((2,))]),
        out_shape=jax.ShapeDtypeStruct((N, D_MODEL), F32),
        compiler_params=_cparams(("arbitrary",)),
        name="combine",
    )(pos_flat, h1, gates, ln_g[None, :], ln_b[None, :], ys)


def _pick(n, prefs):
    for p in prefs:
        if n % p == 0:
            return p
    raise ValueError(f"no tile size in {prefs} divides {n}")


def _moe_layout(idx, rank, counts, tr):
    tiles = (counts + tr - 1) // tr
    tile_end = jnp.cumsum(tiles)
    offs = (tile_end - tiles) * tr
    hot = idx[:, :, None] == jnp.arange(N_EXPERTS, dtype=I32)[None, None, :]
    pos = jnp.sum(jnp.where(hot, offs[None, None, :], 0), axis=-1) + rank
    n_tiles = (idx.shape[0] * TOP_K) // tr + N_EXPERTS
    tile_id = jnp.arange(n_tiles, dtype=I32)
    tile_expert = jnp.sum((tile_end[None, :] <= tile_id[:, None]).astype(I32), axis=1)
    tile_expert = jnp.minimum(tile_expert, N_EXPERTS - 1).astype(I32)
    tail_row = jnp.where(tiles > 0, (tile_end - 1) * tr, -1).astype(I32)
    return pos.reshape(-1).astype(I32), tile_expert, tile_end[-1:].astype(I32), tail_row, n_tiles * tr


def kernel(x, ln_in_g, ln_in_b, w_in, b_in, conv_w, conv_b, m_norm_g, w_out, ln1_g, ln1_b, w_router, b_router,
           w_gate, b_gate, w_up, b_up, w_down, b_down, ln2_g, ln2_b):
    B, T, D = x.shape
    depth = w_in.shape[0]
    assert D == D_MODEL and T % 256 == 0 and depth == 1
    alpha = (2.0 * depth) ** 0.25
    N = B * T
    tm_proj = _pick(T, (512, 256))
    L = 256
    TQ_SEL = 256
    TQ_ATT, TK_ATT = 512, _pick(T, (1024, 512, 256))
    tm_out = _pick(N, (512, 256))
    tm_disp = _pick(N, (512, 256))
    tm_comb = _pick(N, (256,))
    tr = 256

    h = x.reshape(N, D)
    for l in range(depth):
        wp, bp = _pack_in_weights(w_in, b_in, l)
        (h0, mq, mk, mv, mo_sig, aq, ak, av, xq, gm_sig, ga_sig, xk, small, small_t) = _in_proj(
            h, ln_in_g, ln_in_b, wp, bp, T, tm_proj)
        ym = _mlstm(mq, mk, mv, mo_sig, gm_sig, small, small_t, conv_w[l], conv_b[l], m_norm_g[l], B, T, L)
        mask = _select(xq, xk, small_t, B, T, TQ_SEL)
        ya = _attn(aq, ak, av, mask, ga_sig, B, T, TQ_ATT, TK_ATT)
        h1, gates, idx, rank, counts = _out_proj(ym, ya, h0, w_out[l].astype(BF16), ln1_g[l], ln1_b[l],
                                                 w_router[l], b_router[l], tm_out, alpha)
        pos, tile_expert, n_used, tail_row, n_rows = _moe_layout(idx[:, :TOP_K], rank[:, :TOP_K],
                                                                 counts[0, :N_EXPERTS].astype(I32), tr)
        xs = _dispatch(h1, pos, tail_row, n_used, n_rows, tm_disp, tr)
        ys = _experts(xs, tile_expert, n_used, w_gate[l], b_gate[l], w_up[l], b_up[l], w_down[l], b_down[l], tr)
        h = _combine(h1, gates, ys, pos, ln2_g[l], ln2_b[l], tm_comb, alpha)
    return h.reshape(B, T, D)
```

```python
import functools
import math

import jax
import jax.numpy as jnp
from jax import lax
from jax.experimental import pallas as pl
from jax.experimental.pallas import tpu as pltpu

F32 = jnp.float32
BF16 = jnp.bfloat16
I32 = jnp.int32

D_MODEL = 1024
CHUNK = 64
M_HEADS = 4
M_HEAD_DIM = D_MODEL // M_HEADS
M_WIDTH = M_HEADS * M_HEAD_DIM
CONV_WIDTH = 4
A_HEADS = 8
A_HEAD_DIM = D_MODEL // A_HEADS
A_WIDTH = A_HEADS * A_HEAD_DIM
IDX_HEADS = 8
IDX_DIM = 64
IDX_WIDTH = IDX_HEADS * IDX_DIM
TOPK_MAX = 256
ROPE_THETA = 10000.0
N_EXPERTS = 32
TOP_K = 4
D_FF = D_MODEL
SWIGLU_LIMIT = 7.0
SWIGLU_ALPHA = 1.702
LN_EPS = 1e-5

LANES = 128
NEG_BIG = -1e30
F32_TINY = 1.1754943508222875e-38
LOG2_E = 1.4426950408889634
ROUTE_W = 8
DMA_UNROLL = 8
VMEM_LIMIT = 56 * 1024 * 1024

_BIG_GROUPS = ("m_q", "m_k", "m_v", "m_o", "a_q", "a_k", "a_v", "x_q", "g_m", "g_a")
_GROUP_WIDTH = dict(m_q=M_WIDTH, m_k=M_WIDTH, m_v=M_WIDTH, m_o=M_WIDTH, a_q=A_WIDTH, a_k=A_WIDTH,
                    a_v=A_WIDTH, x_q=IDX_WIDTH, g_m=D_MODEL, g_a=D_MODEL, x_k=LANES, small=LANES)
_GROUP_ORDER = _BIG_GROUPS + ("x_k", "small")
_GROUP_START = {}
_off = 0
for _g in _GROUP_ORDER:
    _GROUP_START[_g] = _off
    _off += _GROUP_WIDTH[_g]
PACKED_WIDTH = _off


def _cparams(sem, vmem=VMEM_LIMIT):
    return pltpu.CompilerParams(dimension_semantics=sem, vmem_limit_bytes=vmem)


def _layer_norm(x, g, b):
    mu = jnp.mean(x, axis=-1, keepdims=True)
    xc = x - mu
    var = jnp.mean(xc * xc, axis=-1, keepdims=True)
    return xc * lax.rsqrt(var + LN_EPS) * g + b


def _sigmoid(x):
    return 1.0 / (1.0 + jnp.exp(-x))


def _in_proj_kernel(x_ref, g_ref, b_ref, w_ref, bias_ref, cosa_ref, sina_ref, cosb_ref, sinlo_ref, sinhi_ref,
                    h0_ref, mq_ref, mk_ref, mv_ref, mo_ref, aq_ref, ak_ref, av_ref, xq_ref, gm_ref, ga_ref,
                    xk_ref, small_ref, small_t_ref):
    h0 = _layer_norm(x_ref[...], g_ref[...], b_ref[...])
    h0_ref[...] = h0
    hb = h0.astype(BF16)

    def proj(name, c0=0, width=None):
        start = _GROUP_START[name] + c0
        width = _GROUP_WIDTH[name] if width is None else width
        y = jnp.dot(hb, w_ref[:, start:start + width], preferred_element_type=F32)
        return y + bias_ref[:, start:start + width]

    def rope_full(y):
        return y * cosa_ref[...] + pltpu.roll(y, 64, axis=1) * sina_ref[...]

    def rope_half(y):
        return (y * cosb_ref[...] + pltpu.roll(y, 96, axis=1) * sinlo_ref[...]
                + pltpu.roll(y, 32, axis=1) * sinhi_ref[...])

    mq_ref[...] = proj("m_q").astype(BF16)
    mk_ref[...] = proj("m_k").astype(BF16)
    mv_ref[...] = proj("m_v").astype(BF16)
    mo_ref[...] = _sigmoid(proj("m_o")).astype(BF16)
    gm_ref[...] = _sigmoid(proj("g_m")).astype(BF16)
    ga_ref[...] = _sigmoid(proj("g_a")).astype(BF16)
    av_ref[...] = proj("a_v").astype(BF16)
    q_scale = A_HEAD_DIM ** -0.5 * LOG2_E
    yq, yk = proj("a_q"), proj("a_k")
    for h in range(A_HEADS):
        sl = slice(h * LANES, (h + 1) * LANES)
        aq_ref[:, sl] = (rope_full(yq[:, sl]) * q_scale).astype(BF16)
        ak_ref[:, sl] = rope_full(yk[:, sl]).astype(BF16)
    yx = proj("x_q")
    for c in range(IDX_WIDTH // LANES):
        sl = slice(c * LANES, (c + 1) * LANES)
        xq_ref[:, sl] = rope_half(yx[:, sl]).astype(BF16)
    tail = proj("x_k", 0, 2 * LANES)
    xk_ref[...] = rope_half(tail[:, 0:LANES]).astype(BF16)
    small = tail[:, LANES:2 * LANES]
    small_ref[...] = small
    small_t_ref[...] = small.T


def _rope_tables(T):
    pos = jnp.arange(T, dtype=F32)[:, None]
    half_a = A_HEAD_DIM // 2
    inv_a = ROPE_THETA ** (-jnp.arange(half_a, dtype=F32) / half_a)
    ang_a = pos * inv_a[None, :]
    cos_a, sin_a = jnp.cos(ang_a), jnp.sin(ang_a)
    cosa = jnp.concatenate([cos_a, cos_a], axis=1)
    sina = jnp.concatenate([-sin_a, sin_a], axis=1)
    half_b = IDX_DIM // 2
    inv_b = ROPE_THETA ** (-jnp.arange(half_b, dtype=F32) / half_b)
    ang_b = pos * inv_b[None, :]
    cos_b, sin_b = jnp.cos(ang_b), jnp.sin(ang_b)
    zero = jnp.zeros_like(sin_b)
    cosb = jnp.concatenate([cos_b, cos_b, cos_b, cos_b], axis=1)
    sinlo = jnp.concatenate([-sin_b, zero, -sin_b, zero], axis=1)
    sinhi = jnp.concatenate([zero, sin_b, zero, sin_b], axis=1)
    return cosa, sina, cosb, sinlo, sinhi


def _pack_in_weights(w_all, b_all, l):
    sizes = (M_WIDTH, M_WIDTH, M_WIDTH, M_WIDTH, M_HEADS, M_HEADS, A_WIDTH, A_WIDTH, A_WIDTH,
             IDX_WIDTH, IDX_DIM, IDX_HEADS, D_MODEL, D_MODEL)
    names = ("m_q", "m_k", "m_v", "m_o", "m_i", "m_f", "a_q", "a_k", "a_v", "x_q", "x_k", "x_w", "g_m", "g_a")
    parts, start = {}, 0
    for n, s in zip(names, sizes):
        parts[n] = (w_all[l, :, start:start + s].astype(BF16), b_all[l, start:start + s].astype(F32))
        start += s
    rows = w_all.shape[1]

    def pad(n_cols):
        return jnp.zeros((rows, n_cols), BF16), jnp.zeros((n_cols,), F32)

    order = [parts[n] for n in _BIG_GROUPS]
    order += [parts["x_k"], pad(LANES - IDX_DIM)]
    order += [parts["m_i"], parts["m_f"], parts["x_w"], pad(LANES - 2 * M_HEADS - IDX_HEADS)]
    wp = jnp.concatenate([o[0] for o in order], axis=1)
    bp = jnp.concatenate([o[1] for o in order], axis=0)[None, :]
    return wp, bp


def _in_proj(x2, ln_g, ln_b, wp, bp, T, tm):
    N = x2.shape[0]
    nt = T // tm
    tables = _rope_tables(T)
    row = lambda i: (i, 0)
    const = lambda i: (0, 0)
    tab = lambda i: (i % nt, 0)
    widths = [D_MODEL, M_WIDTH, M_WIDTH, M_WIDTH, M_WIDTH, A_WIDTH, A_WIDTH, A_WIDTH, IDX_WIDTH, D_MODEL, D_MODEL,
              LANES, LANES]
    dtypes = [F32] + [BF16] * 11 + [F32]
    return pl.pallas_call(
        _in_proj_kernel,
        grid=(N // tm,),
        in_specs=[pl.BlockSpec((tm, D_MODEL), row),
                  pl.BlockSpec((1, D_MODEL), const), pl.BlockSpec((1, D_MODEL), const),
                  pl.BlockSpec((D_MODEL, PACKED_WIDTH), const, pipeline_mode=pl.Buffered(1)),
                  pl.BlockSpec((1, PACKED_WIDTH), const)]
                 + [pl.BlockSpec((tm, LANES), tab)] * 5,
        out_specs=[pl.BlockSpec((tm, wd), row) for wd in widths] + [pl.BlockSpec((LANES, tm), lambda i: (0, i))],
        out_shape=[jax.ShapeDtypeStruct((N, wd), dt) for wd, dt in zip(widths, dtypes)]
                  + [jax.ShapeDtypeStruct((LANES, N), F32)],
        compiler_params=_cparams(("parallel",)),
        name="in_proj",
    )(x2, ln_g[None, :], ln_b[None, :], wp, bp, *tables)


def _log_sigmoid(x):
    return jnp.minimum(x, 0.0) - jnp.log(1.0 + jnp.exp(-jnp.abs(x)))


def _split3(x):
    a1 = x.astype(BF16)
    r1 = x - a1.astype(F32)
    a2 = r1.astype(BF16)
    a3 = (r1 - a2.astype(F32)).astype(BF16)
    return a1, a2, a3


def _mlstm_kernel(q_ref, k_ref, v_ref, o_ref, gm_ref, gcol_ref, grow_ref, cwq_ref, cwk_ref, cbq_ref, cbk_ref,
                  ng_ref, out_ref, qprev, kprev, shift_ref, c_state, n_state, m_state, *, L):
    c = pl.program_id(2)
    halo = 8
    r = lax.broadcasted_iota(I32, (L, L), 0)
    s = lax.broadcasted_iota(I32, (L, L), 1)
    causal = r >= s

    @pl.when(c == 0)
    def _():
        qprev[...] = jnp.zeros_like(qprev)
        kprev[...] = jnp.zeros_like(kprev)
        c_state[...] = jnp.zeros_like(c_state)
        n_state[...] = jnp.zeros_like(n_state)
        m_state[...] = jnp.zeros_like(m_state)
        for j in range(1, CONV_WIDTH):
            shift_ref[j - 1] = jnp.where(r - s == j, 1.0, 0.0).astype(BF16)
        shift_ref[CONV_WIDTH - 1] = jnp.where(causal, 1.0, 0.0).astype(BF16)
        shift_ref[CONV_WIDTH] = jnp.where(r <= s, 1.0, 0.0).astype(BF16)

    row8 = lax.broadcasted_iota(I32, (halo, M_HEAD_DIM), 0)

    def conv_silu(prev_ref, x_ref, w_ref, b_ref):
        xb = x_ref[...]
        x = xb.astype(F32)
        acc = b_ref[...] + x * w_ref[CONV_WIDTH - 1:CONV_WIDTH, :]
        prev = prev_ref[...]
        head = jnp.zeros((halo, M_HEAD_DIM), F32)
        for j in range(1, CONV_WIDTH):
            wj = w_ref[CONV_WIDTH - 1 - j:CONV_WIDTH - j, :]
            acc = acc + jnp.dot(shift_ref[j - 1], xb, preferred_element_type=F32) * wj
            head = head + jnp.where(row8 < j, pltpu.roll(prev, j, axis=0), 0.0) * wj
        acc = jnp.concatenate([acc[0:halo, :] + head, acc[halo:, :]], axis=0)
        prev_ref[...] = x[L - halo:L, :]
        return acc * _sigmoid(acc)

    qc = conv_silu(qprev, q_ref, cwq_ref, cbq_ref) * (M_HEAD_DIM ** -0.5)
    kc = conv_silu(kprev, k_ref, cwk_ref, cbk_ref)
    v = v_ref[...]

    hd = pl.program_id(1)
    gcol = gcol_ref[...]
    lane = lax.broadcasted_iota(I32, (L, LANES), 1)
    i_col = jnp.sum(jnp.where(lane == hd, gcol, 0.0), axis=1, keepdims=True)
    lf_col = _log_sigmoid(jnp.sum(jnp.where(lane == hd + M_HEADS, gcol, 0.0), axis=1, keepdims=True))
    i_row = grow_ref[pl.ds(hd, 1), :]
    lf_row = _log_sigmoid(grow_ref[pl.ds(hd + M_HEADS, 1), :])

    b_col = sum(jnp.dot(shift_ref[CONV_WIDTH - 1], part, preferred_element_type=F32)
                for part in _split3(jnp.broadcast_to(lf_col, (L, LANES))))[:, 0:1]
    b_row = sum(jnp.dot(part, shift_ref[CONV_WIDTH], preferred_element_type=F32)
                for part in _split3(jnp.broadcast_to(lf_row, (8, L))))[0:1, :]

    m_prev = m_state[...]
    dm = jnp.where(causal, b_col - b_row + i_row, -jnp.inf)
    m_inter = b_col + m_prev
    m_t = jnp.maximum(m_inter, jnp.max(dm, axis=1, keepdims=True))
    qb, kb = qc.astype(BF16), kc.astype(BF16)
    qk = lax.dot_general(qb, kb, (((1,), (1,)), ((), ())), preferred_element_type=F32)
    sm = qk * jnp.exp(dm - m_t)
    w_inter = jnp.exp(m_inter - m_t)
    cb = c_state[...].astype(BF16)
    num = (jnp.dot(sm.astype(BF16), v, preferred_element_type=F32)
           + w_inter * jnp.dot(qb, cb, preferred_element_type=F32))
    den = (jnp.sum(sm, axis=1, keepdims=True)
           + w_inter * jnp.sum(qc * n_state[...], axis=1, keepdims=True))
    hh = num / jnp.maximum(jnp.abs(den), jnp.exp(-m_t))

    b_last = b_col[L - 1:L, :]
    g_row = b_last - b_row + i_row
    m_new = jnp.maximum(b_last + m_prev, jnp.max(g_row, axis=1, keepdims=True))
    wg_col = jnp.exp(b_last - b_col + i_col - m_new)
    decay = jnp.exp(b_last + m_prev - m_new)
    kw = kc * wg_col
    kv = lax.dot_general(kw.astype(BF16), v, (((0,), (0,)), ((), ())), preferred_element_type=F32)
    c_state[...] = decay * c_state[...] + kv
    n_state[...] = decay * n_state[...] + jnp.sum(kw, axis=0, keepdims=True)
    m_state[...] = m_new

    mu = jnp.mean(hh, axis=1, keepdims=True)
    hc = hh - mu
    var = jnp.mean(hc * hc, axis=1, keepdims=True)
    hn = hc * lax.rsqrt(var + LN_EPS) * ng_ref[...]
    out_ref[...] = (hn * o_ref[...].astype(F32) * gm_ref[...].astype(F32)).astype(BF16)


def _mlstm(mq, mk, mv, mo_sig, gm_sig, small, small_t, conv_w, conv_b, norm_g, B, T, L):
    nL = T // L
    blk = lambda b, h, c: (b * nL + c, h)
    head = lambda b, h, c: (0, h)
    kern = functools.partial(_mlstm_kernel, L=L)
    return pl.pallas_call(
        kern,
        grid=(B, M_HEADS, nL),
        in_specs=[pl.BlockSpec((L, M_HEAD_DIM), blk)] * 5
                 + [pl.BlockSpec((L, LANES), lambda b, h, c: (b * nL + c, 0)),
                    pl.BlockSpec((2 * M_HEADS, L), lambda b, h, c: (0, b * nL + c)),
                    pl.BlockSpec((CONV_WIDTH, M_HEAD_DIM), head),
                    pl.BlockSpec((CONV_WIDTH, M_HEAD_DIM), head),
                    pl.BlockSpec((1, M_HEAD_DIM), head),
                    pl.BlockSpec((1, M_HEAD_DIM), head),
                    pl.BlockSpec((1, M_HEAD_DIM), head)],
        out_specs=pl.BlockSpec((L, M_HEAD_DIM), blk),
        out_shape=jax.ShapeDtypeStruct((B * T, M_WIDTH), BF16),
        scratch_shapes=[pltpu.VMEM((8, M_HEAD_DIM), F32), pltpu.VMEM((8, M_HEAD_DIM), F32),
                        pltpu.VMEM((CONV_WIDTH + 1, L, L), BF16),
                        pltpu.VMEM((M_HEAD_DIM, M_HEAD_DIM), F32), pltpu.VMEM((1, M_HEAD_DIM), F32),
                        pltpu.VMEM((1, 1), F32)],
        compiler_params=_cparams(("parallel", "parallel", "arbitrary")),
        name="mlstm",
    )(mq, mk, mv, mo_sig, gm_sig, small, small_t,
      conv_w[:, :M_WIDTH], conv_w[:, M_WIDTH:], conv_b[None, :M_WIDTH], conv_b[None, M_WIDTH:], norm_g[None, :])


def _select_kernel(xq_ref, xk_ref, xw_ref, mask_ref, sc_ref, ti_ref, jsel_ref, *raw_refs, TQ, T, top_k):
    qi = pl.program_id(1)
    nkt = qi + 1
    SUB = 8
    NACC = 4
    kf = float(top_k)
    w_all = xw_ref[...] * ((IDX_DIM ** -0.5) * (IDX_HEADS ** -0.5))

    key_chunk = lax.broadcasted_iota(I32, (TQ, TQ), 0) // CHUNK
    qry_chunk = lax.broadcasted_iota(I32, (TQ, TQ), 1) // CHUNK
    diag_ok = key_chunk <= qry_chunk

    def fold(op, acc, tile):
        for i in range(TQ // SUB):
            acc = op(acc, tile[i * SUB:(i + 1) * SUB, :])
        return acc

    def products(kt, raw_ref):
        koff = pl.multiple_of(jnp.minimum(kt, nkt - 1) * TQ, TQ)
        kblk = xk_ref[pl.ds(koff, TQ), 0:IDX_DIM]
        for h in range(IDX_HEADS):
            qh = xq_ref[:, h * IDX_DIM:(h + 1) * IDX_DIM]
            raw_ref[h] = lax.dot_general(kblk, qh, (((1,), (1,)), ((), ())), preferred_element_type=F32)

    def combine(kt, raw_ref, mx, mn):
        kt = jnp.minimum(kt, nkt - 1)
        koff = pl.multiple_of(kt * TQ, TQ)
        acc = jnp.zeros((TQ, TQ), F32)
        for h in range(IDX_HEADS):
            acc = acc + jnp.maximum(raw_ref[h], 0.0) * w_all[h:h + 1, :]
        ok = jnp.logical_or(kt < qi, diag_ok)
        sc_ref[pl.ds(koff, TQ), :] = jnp.where(ok, acc, NEG_BIG)
        mx = fold(jnp.maximum, mx, jnp.where(ok, acc, NEG_BIG))
        mn = fold(jnp.minimum, mn, jnp.where(ok, acc, -NEG_BIG))
        return mx, mn

    def score_pair(j, carry):
        mx, mn = carry
        products(2 * j + 1, raw_refs[1])
        mx, mn = combine(2 * j, raw_refs[0], mx, mn)
        products(2 * j + 2, raw_refs[0])
        return combine(2 * j + 1, raw_refs[1], mx, mn)

    products(0, raw_refs[0])
    mx, mn = lax.fori_loop(0, (nkt + 1) // 2, score_pair,
                           (jnp.full((SUB, TQ), NEG_BIG, F32), jnp.full((SUB, TQ), -NEG_BIG, F32)))
    rep = lambda v: jnp.broadcast_to(v, (SUB, TQ))
    mx = rep(jnp.max(mx, axis=0, keepdims=True))
    mn = rep(jnp.min(mn, axis=0, keepdims=True))

    def count(pred, ref=sc_ref, lanes=slice(0, TQ)):
        width = lanes.stop - lanes.start

        def body(kt, accs):
            koff = pl.multiple_of(kt * TQ, TQ)
            tile = ref[pl.ds(koff, TQ), lanes]
            accs = list(accs)
            for i in range(TQ // SUB):
                sv = tile[i * SUB:(i + 1) * SUB, :]
                accs[i % NACC] = accs[i % NACC] + jnp.where(pred(sv, koff + i * SUB), 1.0, 0.0)
            return tuple(accs)

        accs = lax.fori_loop(0, nkt, body, tuple(jnp.zeros((SUB, width), F32) for _ in range(NACC)))
        tot = accs[0]
        for a in accs[1:]:
            tot = tot + a
        return jnp.broadcast_to(jnp.sum(tot, axis=0, keepdims=True), (SUB, width))

    def any_query(flag):
        return jnp.max(jnp.where(flag, 1.0, 0.0)) > 0.0

    q_id = qi * TQ + lax.broadcasted_iota(I32, (SUB, TQ), 1)
    n_adm = ((q_id // CHUNK + 1) * CHUNK).astype(F32)
    lo0 = mn
    hi0 = mx + jnp.maximum(jnp.abs(mx), 1e-30) * 1e-3
    c_zero = count(lambda sv, off: sv >= 0.0)
    c_pos = count(lambda sv, off: sv >= F32_TINY)
    pos_side = jnp.logical_and(c_pos >= kf, lo0 < F32_TINY)
    zero_hit = jnp.logical_and(c_pos < kf, c_zero >= kf)
    neg_side = jnp.logical_and(c_zero < kf, hi0 > 0.0)
    lo1 = jnp.where(pos_side, F32_TINY, jnp.where(zero_hit, 0.0, lo0))
    clo1 = jnp.where(pos_side, c_pos, jnp.where(zero_hit, c_zero, n_adm))
    hi1 = jnp.where(zero_hit, F32_TINY, jnp.where(neg_side, 0.0, hi0))
    chi1 = jnp.where(zero_hit, c_pos, jnp.where(neg_side, c_zero, 0.0))

    def active(lo, hi, clo):
        mid = 0.5 * lo + 0.5 * hi
        return jnp.logical_and(clo > kf, jnp.logical_and(mid > lo, mid < hi)), mid

    def halve(lo, hi, clo, chi):
        act, mid = active(lo, hi, clo)
        cnt = count(lambda sv, off: sv >= mid)
        ge = cnt >= kf
        up = jnp.logical_and(act, ge)
        dn = jnp.logical_and(act, jnp.logical_not(ge))
        return (jnp.where(up, mid, lo), jnp.where(dn, mid, hi), jnp.where(up, cnt, clo), jnp.where(dn, cnt, chi))

    def cond(carry):
        return jnp.logical_and(carry[1], carry[0] < 200)

    def body(carry):
        it, _, lo, hi, clo, chi = carry
        lo, hi, clo, chi = halve(*halve(lo, hi, clo, chi))
        return it + 1, any_query(active(lo, hi, clo)[0]), lo, hi, clo, chi

    _, _, lo, hi, clo, chi = lax.while_loop(
        cond, body, (jnp.int32(0), any_query(active(lo1, hi1, clo1)[0]), lo1, hi1, clo1, chi1))

    def key_index(off, width=TQ):
        return (off + lax.broadcasted_iota(I32, (SUB, width), 0)).astype(F32)

    def write_mask(keep):
        def write(kt, _):
            koff = pl.multiple_of(kt * TQ, TQ)
            tile = sc_ref[pl.ds(koff, TQ), :]
            slabs = []
            for i in range(TQ // SUB):
                sv = tile[i * SUB:(i + 1) * SUB, :]
                slabs.append(jnp.where(keep(sv, koff + i * SUB), 1, 0))
            mask_ref[:, pl.ds(koff, TQ)] = jnp.concatenate(slabs, axis=0).T.astype(jnp.int8)
            return 0

        lax.fori_loop(0, nkt, write, 0)

    def resolve_ties(lanes):
        width = lanes.stop - lanes.start
        lo_g, hi_g, clo_g, chi_g = lo[:, lanes], hi[:, lanes], clo[:, lanes], chi[:, lanes]
        tie = clo_g > kf

        @pl.when(any_query(tie))
        def _():
            need = kf - chi_g
            n_tie = clo_g - chi_g

            def stage(kt, _):
                koff = pl.multiple_of(kt * TQ, TQ)
                tile = sc_ref[pl.ds(koff, TQ), lanes]
                slabs = []
                for i in range(TQ // SUB):
                    sv = tile[i * SUB:(i + 1) * SUB, :]
                    slabs.append(jnp.where(jnp.logical_and(sv >= lo_g, sv < hi_g),
                                           key_index(koff + i * SUB, width), 2.0 * T))
                ti_ref[pl.ds(koff, TQ), lanes] = jnp.concatenate(slabs, axis=0)
                return 0

            lax.fori_loop(0, nkt, stage, 0)

            def jactive(jlo, jhi):
                return jnp.logical_and(tie, jhi - jlo > 1.5)

            def jcond(carry):
                return jnp.logical_and(carry[1], carry[0] < 64)

            def jbody(carry):
                it, _, jlo, jhi, cjlo, cjhi = carry
                act = jactive(jlo, jhi)
                frac = (need - cjlo) / jnp.maximum(cjhi - cjlo, 1.0)
                jint = jnp.ceil(jlo + (jhi - jlo) * frac)
                jmid = jnp.floor(0.5 * (jlo + jhi))
                even = jnp.full((SUB, width), (it % 2 == 0).astype(F32), F32) > 0.5
                j = jnp.where(even, jnp.clip(jint, jlo + 1.0, jhi - 1.0), jmid)
                cnt = count(lambda tv, off: tv < j, ti_ref, lanes)
                ge = cnt >= need
                up = jnp.logical_and(act, ge)
                dn = jnp.logical_and(act, jnp.logical_not(ge))
                jhi = jnp.where(up, j, jhi)
                cjhi = jnp.where(up, cnt, cjhi)
                jlo = jnp.where(dn, j, jlo)
                cjlo = jnp.where(dn, cnt, cjlo)
                return it + 1, any_query(jactive(jlo, jhi)), jlo, jhi, cjlo, cjhi

            jlo0 = jnp.zeros((SUB, width), F32)
            jhi0 = jnp.full((SUB, width), float(T), F32)
            _, _, _, jhi, _, _ = lax.while_loop(
                jcond, jbody, (jnp.int32(0), any_query(jactive(jlo0, jhi0)), jlo0, jhi0, jlo0, n_tie))
            jsel_ref[:, lanes] = jnp.where(tie, jhi, float(T))

    has_tie = any_query(clo > kf)

    @pl.when(jnp.logical_not(has_tie))
    def _():
        write_mask(lambda sv, off: sv >= lo)

    @pl.when(has_tie)
    def _():
        jsel_ref[...] = jnp.full((SUB, TQ), float(T), F32)
        for g in range(TQ // LANES):
            resolve_ties(slice(g * LANES, (g + 1) * LANES))
        jsel = jsel_ref[...]
        write_mask(lambda sv, off: jnp.logical_and(
            sv >= lo, jnp.logical_or(sv >= hi, key_index(off) < jsel)))

    def zero_tile(kt, _):
        mask_ref[:, pl.ds(pl.multiple_of(kt * TQ, TQ), TQ)] = jnp.zeros((TQ, TQ), jnp.int8)
        return 0

    lax.fori_loop(nkt, T // TQ, zero_tile, 0)


def _select(xq, xk, small_t, B, T, TQ):
    nq = T // TQ
    top_k = min(TOPK_MAX, T // 4)
    assert 2 * M_HEADS == IDX_HEADS
    kern = functools.partial(_select_kernel, TQ=TQ, T=T, top_k=top_k)
    return pl.pallas_call(
        kern,
        grid=(B, nq),
        in_specs=[pl.BlockSpec((TQ, IDX_WIDTH), lambda b, q: (b * nq + q, 0)),
                  pl.BlockSpec((T, LANES), lambda b, q: (b, 0)),
                  pl.BlockSpec((IDX_HEADS, TQ), lambda b, q: (1, b * nq + q))],
        out_specs=pl.BlockSpec((TQ, T), lambda b, q: (b * nq + q, 0)),
        out_shape=jax.ShapeDtypeStruct((B * T, T), jnp.int8),
        scratch_shapes=[pltpu.VMEM((T, TQ), F32), pltpu.VMEM((T, TQ), F32), pltpu.VMEM((8, TQ), F32),
                        pltpu.VMEM((IDX_HEADS, TQ, TQ), F32), pltpu.VMEM((IDX_HEADS, TQ, TQ), F32)],
        compiler_params=_cparams(("parallel", "parallel")),
        name="select",
    )(xq, xk, small_t)


def _attn_kernel(qtab_ref, ktab_ref, q_ref, k_ref, v_ref, mask_ref, g_ref, out_ref, bias_ref, *head_refs, TQ, TK):
    vext_refs = head_refs[0:A_HEADS]
    m_refs = head_refs[A_HEADS:2 * A_HEADS]
    acc_refs = head_refs[2 * A_HEADS:3 * A_HEADS]
    step = pl.program_id(1)
    qi = qtab_ref[step]
    kt = ktab_ref[step]
    last = ((qi + 1) * TQ - 1) // TK

    @pl.when(kt == 0)
    def _():
        for h in range(A_HEADS):
            m_refs[h][...] = jnp.full((TQ, LANES), NEG_BIG, F32)
            acc_refs[h][...] = jnp.zeros((TQ, 2 * A_HEAD_DIM), F32)
            vext_refs[h][:, A_HEAD_DIM:] = jnp.ones((TK, A_HEAD_DIM), BF16)

    bias_ref[...] = ((mask_ref[...].astype(F32) - 1.0) * (-NEG_BIG)).astype(BF16)
    for h in range(A_HEADS):
        sl = slice(h * A_HEAD_DIM, (h + 1) * A_HEAD_DIM)
        vext_refs[h][:, 0:A_HEAD_DIM] = v_ref[:, sl]
        s = lax.dot_general(q_ref[:, sl], k_ref[:, sl], (((1,), (1,)), ((), ())), preferred_element_type=F32)
        sb = s.astype(BF16) + bias_ref[...]
        m_old = m_refs[h][...]
        mx = jnp.max(sb, axis=1, keepdims=True).astype(F32)
        m_new = jnp.maximum(m_old, jnp.broadcast_to(mx, (TQ, LANES)))
        alpha = jnp.exp2(m_old - m_new)
        p = jnp.exp2(sb - m_new[:, 0:1].astype(BF16))
        pv = jnp.dot(p, vext_refs[h][...], preferred_element_type=F32)
        acc_refs[h][...] = jnp.concatenate([alpha, alpha], axis=1) * acc_refs[h][...] + pv
        m_refs[h][...] = m_new

    @pl.when(kt == last)
    def _():
        for h in range(A_HEADS):
            sl = slice(h * A_HEAD_DIM, (h + 1) * A_HEAD_DIM)
            num = acc_refs[h][:, 0:A_HEAD_DIM]
            den = acc_refs[h][:, A_HEAD_DIM:]
            out_ref[:, sl] = (num / den * g_ref[:, sl].astype(F32)).astype(BF16)


def _attn(aq, ak, av, mask, ga_sig, B, T, TQ, TK):
    nq, nk = T // TQ, T // TK
    pairs = [(q, k) for q in range(nq) for k in range(((q + 1) * TQ - 1) // TK + 1)]
    qtab = jnp.asarray([p[0] for p in pairs], I32)
    ktab = jnp.asarray([p[1] for p in pairs], I32)

    qmap = lambda b, s, qt, kt: (b * nq + qt[s], 0)
    kv_map = lambda b, s, qt, kt: (b * nk + kt[s], 0)
    mask_map = lambda b, s, qt, kt: (b * nq + qt[s], kt[s])
    kern = functools.partial(_attn_kernel, TQ=TQ, TK=TK)
    return pl.pallas_call(
        kern,
        grid_spec=pltpu.PrefetchScalarGridSpec(
            num_scalar_prefetch=2,
            grid=(B, len(pairs)),
            in_specs=[pl.BlockSpec((TQ, A_WIDTH), qmap),
                      pl.BlockSpec((TK, A_WIDTH), kv_map),
                      pl.BlockSpec((TK, A_WIDTH), kv_map),
                      pl.BlockSpec((TQ, TK), mask_map),
                      pl.BlockSpec((TQ, A_WIDTH), qmap)],
            out_specs=pl.BlockSpec((TQ, A_WIDTH), qmap),
            scratch_shapes=[pltpu.VMEM((TQ, TK), BF16)]
                           + [pltpu.VMEM((TK, 2 * A_HEAD_DIM), BF16)] * A_HEADS
                           + [pltpu.VMEM((TQ, LANES), F32)] * A_HEADS
                           + [pltpu.VMEM((TQ, 2 * A_HEAD_DIM), F32)] * A_HEADS),
        out_shape=jax.ShapeDtypeStruct((B * T, A_WIDTH), BF16),
        compiler_params=_cparams(("parallel", "arbitrary")),
        name="attn",
    )(qtab, ktab, aq, ak, av, mask, ga_sig)


def _out_proj_kernel(ym_ref, ya_ref, h0_ref, wo_ref, g_ref, b_ref, wr_hi_ref, wr_lo_ref, br_ref,
                     h1_ref, gate_ref, idx_ref, rank_ref, cnt_ref, carry_ref, *, tm, alpha):
    i = pl.program_id(0)

    @pl.when(i == 0)
    def _():
        carry_ref[...] = jnp.zeros_like(carry_ref)

    merged = (ym_ref[...].astype(F32) + ya_ref[...].astype(F32)).astype(BF16)
    y = jnp.dot(merged, wo_ref[...], preferred_element_type=F32)
    h1 = _layer_norm(alpha * h0_ref[...] + y, g_ref[...], b_ref[...])
    h1_ref[...] = h1

    h_hi = h1.astype(BF16)
    h_lo = (h1 - h_hi.astype(F32)).astype(BF16)
    logits = (jnp.dot(h_hi, wr_hi_ref[...], preferred_element_type=F32)
              + jnp.dot(h_hi, wr_lo_ref[...], preferred_element_type=F32)
              + jnp.dot(h_lo, wr_hi_ref[...], preferred_element_type=F32)) + br_ref[...]
    lane = lax.broadcasted_iota(I32, (tm, LANES), 1)
    vals, hots = [], []
    idx_out = jnp.zeros((tm, LANES), I32)
    work = logits
    for r in range(TOP_K):
        mx = jnp.max(work, axis=1, keepdims=True)
        first = jnp.min(jnp.where(work == mx, lane, LANES), axis=1, keepdims=True)
        hot = lane == first
        vals.append(mx)
        hots.append(hot)
        idx_out = jnp.where(lane == r, first, idx_out)
        work = jnp.where(hot, -jnp.inf, work)
    exps = [jnp.exp(v - vals[0]) for v in vals]
    tot = exps[0] + exps[1] + exps[2] + exps[3]
    gate_out = jnp.zeros((tm, LANES), F32)
    chosen = jnp.zeros((tm, LANES), F32)
    for r in range(TOP_K):
        gate_out = jnp.where(lane == r, exps[r] / tot, gate_out)
        chosen = chosen + jnp.where(hots[r], 1.0, 0.0)

    rr = lax.broadcasted_iota(I32, (tm, tm), 0)
    cc = lax.broadcasted_iota(I32, (tm, tm), 1)
    strict = (rr > cc).astype(BF16)
    before = jnp.dot(strict, chosen.astype(BF16), preferred_element_type=F32) + carry_ref[...]
    rank_out = jnp.zeros((tm, LANES), I32)
    for r in range(TOP_K):
        rk = jnp.sum(jnp.where(hots[r], before, 0.0), axis=1, keepdims=True)
        rank_out = jnp.where(lane == r, rk.astype(I32), rank_out)
    carry_ref[...] = carry_ref[...] + jnp.sum(chosen, axis=0, keepdims=True)

    gate_ref[...] = gate_out[:, 0:ROUTE_W]
    idx_ref[...] = idx_out[:, 0:ROUTE_W]
    rank_ref[...] = rank_out[:, 0:ROUTE_W]
    cnt_ref[...] = carry_ref[...]


def _out_proj(ym, ya, h0, w_out_b, ln_g, ln_b, w_router, b_router, tm, alpha):
    N = ym.shape[0]
    wr = jnp.zeros((D_MODEL, LANES), F32).at[:, :N_EXPERTS].set(w_router)
    wr_hi = wr.astype(BF16)
    wr_lo = (wr - wr_hi.astype(F32)).astype(BF16)
    br = jnp.full((1, LANES), NEG_BIG, F32).at[0, :N_EXPERTS].set(b_router)
    row = lambda i: (i, 0)
    const = lambda i: (0, 0)
    kern = functools.partial(_out_proj_kernel, tm=tm, alpha=alpha)
    return pl.pallas_call(
        kern,
        grid=(N // tm,),
        in_specs=[pl.BlockSpec((tm, D_MODEL), row), pl.BlockSpec((tm, D_MODEL), row),
                  pl.BlockSpec((tm, D_MODEL), row),
                  pl.BlockSpec((D_MODEL, D_MODEL), const),
                  pl.BlockSpec((1, D_MODEL), const), pl.BlockSpec((1, D_MODEL), const),
                  pl.BlockSpec((D_MODEL, LANES), const), pl.BlockSpec((D_MODEL, LANES), const),
                  pl.BlockSpec((1, LANES), const)],
        out_specs=[pl.BlockSpec((tm, D_MODEL), row), pl.BlockSpec((tm, ROUTE_W), row),
                   pl.BlockSpec((tm, ROUTE_W), row), pl.BlockSpec((tm, ROUTE_W), row),
                   pl.BlockSpec((1, LANES), const)],
        out_shape=[jax.ShapeDtypeStruct((N, D_MODEL), F32), jax.ShapeDtypeStruct((N, ROUTE_W), F32),
                   jax.ShapeDtypeStruct((N, ROUTE_W), I32), jax.ShapeDtypeStruct((N, ROUTE_W), I32),
                   jax.ShapeDtypeStruct((1, LANES), F32)],
        scratch_shapes=[pltpu.VMEM((1, LANES), F32)],
        compiler_params=_cparams(("arbitrary",)),
        name="out_proj",
    )(ym, ya, h0, w_out_b, ln_g[None, :], ln_b[None, :], wr_hi, wr_lo, br)


def _dispatch_kernel(pos_ref, tail_ref, nused_ref, h_ref, xs_ref, zero_ref, sem, *, tm, tr, n_tiles):
    base = pl.program_id(0) * (tm * TOP_K)

    @pl.when(pl.program_id(0) == 0)
    def _():
        zero_ref[...] = jnp.zeros_like(zero_ref)

        def clear(row0):
            return pltpu.make_async_copy(zero_ref, xs_ref.at[pl.ds(pl.multiple_of(row0, tr), tr), :], sem)

        def start_tail(e, _):
            @pl.when(tail_ref[e] >= 0)
            def _():
                clear(jnp.maximum(tail_ref[e], 0)).start()
            return 0

        def wait_tail(e, _):
            @pl.when(tail_ref[e] >= 0)
            def _():
                clear(jnp.maximum(tail_ref[e], 0)).wait()
            return 0

        def start_unused(t, _):
            clear(t * tr).start()
            return 0

        def wait_unused(t, _):
            clear(t * tr).wait()
            return 0

        lax.fori_loop(0, N_EXPERTS, start_tail, 0)
        lax.fori_loop(nused_ref[0], n_tiles, start_unused, 0)
        lax.fori_loop(0, N_EXPERTS, wait_tail, 0)
        lax.fori_loop(nused_ref[0], n_tiles, wait_unused, 0)

    def copy(t, r):
        dst = pos_ref[base + t * TOP_K + r]
        return pltpu.make_async_copy(h_ref.at[pl.ds(t, 1), :], xs_ref.at[pl.ds(dst, 1), :], sem)

    def start(t, _):
        for r in range(TOP_K):
            copy(t, r).start()
        return 0

    def wait(t, _):
        for r in range(TOP_K):
            copy(t, r).wait()
        return 0

    lax.fori_loop(0, tm, start, 0, unroll=DMA_UNROLL)
    lax.fori_loop(0, tm, wait, 0, unroll=DMA_UNROLL)


def _dispatch(h1, pos_flat, tail_row, n_used, n_rows, tm, tr):
    N = h1.shape[0]
    kern = functools.partial(_dispatch_kernel, tm=tm, tr=tr, n_tiles=n_rows // tr)
    return pl.pallas_call(
        kern,
        grid_spec=pltpu.PrefetchScalarGridSpec(
            num_scalar_prefetch=3,
            grid=(N // tm,),
            in_specs=[pl.BlockSpec((tm, D_MODEL), lambda i, pos, tail, nused: (i, 0))],
            out_specs=pl.BlockSpec(memory_space=pl.ANY),
            scratch_shapes=[pltpu.VMEM((tr, D_MODEL), F32), pltpu.SemaphoreType.DMA(())]),
        out_shape=jax.ShapeDtypeStruct((n_rows, D_MODEL), F32),
        compiler_params=_cparams(("arbitrary",)),
        name="dispatch",
    )(pos_flat, tail_row, n_used, h1)


def _expert_kernel(te_ref, nused_ref, slot_ref, next_ref, xs_ref, wg_hbm, wu_hbm, wd_hbm, bg_ref, bu_ref, bd_ref,
                   ys_ref, wbuf, wgb, wub, wdb, sems):
    i = pl.program_id(0)
    used = i < nused_ref[0]
    fresh = jnp.logical_or(i == 0, te_ref[i] != te_ref[jnp.maximum(i - 1, 0)])

    def fetch(e, s):
        return [pltpu.make_async_copy(w.at[e], wbuf.at[s, j], sems.at[s])
                for j, w in enumerate((wg_hbm, wu_hbm, wd_hbm))]

    @pl.when(i == 0)
    def _():
        for c in fetch(te_ref[0], slot_ref[0]):
            c.start()

    @pl.when(jnp.logical_and(used, fresh))
    def _():
        s = slot_ref[i]
        for c in fetch(te_ref[i], s):
            c.wait()
        wgb[...] = wbuf[s, 0].astype(BF16)
        wub[...] = wbuf[s, 1].astype(BF16)
        wdb[...] = wbuf[s, 2].astype(BF16)

        @pl.when(next_ref[i] >= 0)
        def _():
            for c in fetch(next_ref[i], 1 - s):
                c.start()

    @pl.when(used)
    def _():
        x = xs_ref[...].astype(BF16)
        g = jnp.minimum(jnp.dot(x, wgb[...], preferred_element_type=F32) + bg_ref[0], SWIGLU_LIMIT)
        u = jnp.clip(jnp.dot(x, wub[...], preferred_element_type=F32) + bu_ref[0], -SWIGLU_LIMIT, SWIGLU_LIMIT)
        act = (u + 1.0) * g * _sigmoid(SWIGLU_ALPHA * g)
        ys_ref[...] = jnp.dot(act.astype(BF16), wdb[...], preferred_element_type=F32) + bd_ref[0]

    @pl.when(jnp.logical_not(used))
    def _():
        ys_ref[...] = jnp.zeros_like(ys_ref)


def _experts(xs, tile_expert, n_used, tile_slot, tile_next, w_gate, b_gate, w_up, b_up, w_down, b_down, tr):
    P = xs.shape[0]
    assert D_FF == D_MODEL
    bmap = lambda i, te, nu, sl, nx: (te[i], 0, 0)
    tile = lambda i, te, nu, sl, nx: (i, 0)
    return pl.pallas_call(
        _expert_kernel,
        grid_spec=pltpu.PrefetchScalarGridSpec(
            num_scalar_prefetch=4,
            grid=(P // tr,),
            in_specs=[pl.BlockSpec((tr, D_MODEL), tile),
                      pl.BlockSpec(memory_space=pl.ANY), pl.BlockSpec(memory_space=pl.ANY),
                      pl.BlockSpec(memory_space=pl.ANY),
                      pl.BlockSpec((1, 1, D_FF), bmap), pl.BlockSpec((1, 1, D_FF), bmap),
                      pl.BlockSpec((1, 1, D_MODEL), bmap)],
            out_specs=pl.BlockSpec((tr, D_MODEL), tile),
            scratch_shapes=[pltpu.VMEM((2, 3, D_MODEL, D_FF), F32),
                            pltpu.VMEM((D_MODEL, D_FF), BF16), pltpu.VMEM((D_MODEL, D_FF), BF16),
                            pltpu.VMEM((D_FF, D_MODEL), BF16), pltpu.SemaphoreType.DMA((2,))]),
        out_shape=jax.ShapeDtypeStruct((P, D_MODEL), F32),
        compiler_params=_cparams(("arbitrary",)),
        name="experts",
    )(tile_expert, n_used, tile_slot, tile_next, xs, w_gate, w_up, w_down,
      b_gate[:, None, :], b_up[:, None, :], b_down[:, None, :])


def _combine_kernel(pos_ref, h1_ref, gate_ref, g_ref, b_ref, ys_ref, out_ref, buf, sems, *, tm, alpha):
    i = pl.program_id(0)
    slot = i % 2

    def copy(step, s, t, r):
        src = pos_ref[(step * tm + t) * TOP_K + r]
        return pltpu.make_async_copy(ys_ref.at[pl.ds(src, 1), :], buf.at[s, r, pl.ds(t, 1), :], sems.at[s])

    def start_tile(step, s):
        def start(t, _):
            for r in range(TOP_K):
                copy(step, s, t, r).start()
            return 0

        lax.fori_loop(0, tm, start, 0, unroll=DMA_UNROLL)

    def wait_tile(step, s):
        def wait(t, _):
            for r in range(TOP_K):
                copy(step, s, t, r).wait()
            return 0

        lax.fori_loop(0, tm, wait, 0, unroll=DMA_UNROLL)

    @pl.when(i == 0)
    def _():
        start_tile(0, 0)

    @pl.when(i + 1 < pl.num_programs(0))
    def _():
        start_tile(i + 1, 1 - slot)

    wait_tile(i, slot)
    gates = gate_ref[...]
    moe = gates[:, 0:1] * buf[slot, 0]
    for r in range(1, TOP_K):
        moe = moe + gates[:, r:r + 1] * buf[slot, r]
    out_ref[...] = _layer_norm(alpha * h1_ref[...] + moe, g_ref[...], b_ref[...])


def _combine(h1, gates, ys, pos_flat, ln_g, ln_b, tm, alpha):
    N = h1.shape[0]
    kern = functools.partial(_combine_kernel, tm=tm, alpha=alpha)
    return pl.pallas_call(
        kern,
        grid_spec=pltpu.PrefetchScalarGridSpec(
            num_scalar_prefetch=1,
            grid=(N // tm,),
            in_specs=[pl.BlockSpec((tm, D_MODEL), lambda i, pos: (i, 0)),
                      pl.BlockSpec((tm, ROUTE_W), lambda i, pos: (i, 0)),
                      pl.BlockSpec((1, D_MODEL), lambda i, pos: (0, 0)),
                      pl.BlockSpec((1, D_MODEL), lambda i, pos: (0, 0)),
                      pl.BlockSpec(memory_space=pl.ANY)],
            out_specs=pl.BlockSpec((tm, D_MODEL), lambda i, pos: (i, 0)),
            scratch_shapes=[pltpu.VMEM((2, TOP_K, tm, D_MODEL), F32), pltpu.SemaphoreType.DMA((2,))]),
        out_shape=jax.ShapeDtypeStruct((N, D_MODEL), F32),
        compiler_params=_cparams(("arbitrary",)),
        name="combine",
    )(pos_flat, h1, gates, ln_g[None, :], ln_b[None, :], ys)


def _pick(n, prefs):
    for p in prefs:
        if n % p == 0:
            return p
    raise ValueError(f"no tile size in {prefs} divides {n}")


def _moe_layout(idx, rank, counts, tr):
    tiles = (counts + tr - 1) // tr
    tile_end = jnp.cumsum(tiles)
    offs = (tile_end - tiles) * tr
    hot = idx[:, :, None] == jnp.arange(N_EXPERTS, dtype=I32)[None, None, :]
    pos = jnp.sum(jnp.where(hot, offs[None, None, :], 0), axis=-1) + rank
    n_tiles = (idx.shape[0] * TOP_K) // tr + N_EXPERTS
    tile_id = jnp.arange(n_tiles, dtype=I32)
    tile_expert = jnp.sum((tile_end[None, :] <= tile_id[:, None]).astype(I32), axis=1)
    tile_expert = jnp.minimum(tile_expert, N_EXPERTS - 1).astype(I32)
    tail_row = jnp.where(tiles > 0, (tile_end - 1) * tr, -1).astype(I32)
    e_id = jnp.arange(N_EXPERTS, dtype=I32)
    nonempty = tiles > 0
    ordinal = jnp.cumsum(nonempty.astype(I32)) - nonempty.astype(I32)
    later = jnp.logical_and(e_id[None, :] > e_id[:, None], nonempty[None, :])
    next_e = jnp.min(jnp.where(later, e_id[None, :], N_EXPERTS), axis=1)
    next_e = jnp.where(next_e == N_EXPERTS, -1, next_e)
    mine = tile_expert[:, None] == e_id[None, :]
    tile_slot = jnp.sum(jnp.where(mine, (ordinal % 2)[None, :], 0), axis=1).astype(I32)
    tile_next = jnp.sum(jnp.where(mine, next_e[None, :], 0), axis=1).astype(I32)
    return (pos.reshape(-1).astype(I32), tile_expert, tile_end[-1:].astype(I32), tail_row, tile_slot, tile_next,
            n_tiles * tr)


def kernel(x, ln_in_g, ln_in_b, w_in, b_in, conv_w, conv_b, m_norm_g, w_out, ln1_g, ln1_b, w_router, b_router,
           w_gate, b_gate, w_up, b_up, w_down, b_down, ln2_g, ln2_b):
    B, T, D = x.shape
    depth = w_in.shape[0]
    assert D == D_MODEL and T % 256 == 0 and depth == 1
    alpha = (2.0 * depth) ** 0.25
    N = B * T
    tm_proj = _pick(T, (512, 256))
    L = 256
    TQ_SEL = 256
    TQ_ATT, TK_ATT = 512, _pick(T, (1024, 512, 256))
    tm_out = _pick(N, (512, 256))
    tm_disp = _pick(N, (512, 256))
    tm_comb = _pick(N, (256,))
    tr = 256

    h = x.reshape(N, D)
    for l in range(depth):
        wp, bp = _pack_in_weights(w_in, b_in, l)
        (h0, mq, mk, mv, mo_sig, aq, ak, av, xq, gm_sig, ga_sig, xk, small, small_t) = _in_proj(
            h, ln_in_g, ln_in_b, wp, bp, T, tm_proj)
        ym = _mlstm(mq, mk, mv, mo_sig, gm_sig, small, small_t, conv_w[l], conv_b[l], m_norm_g[l], B, T, L)
        mask = _select(xq, xk, small_t, B, T, TQ_SEL)
        ya = _attn(aq, ak, av, mask, ga_sig, B, T, TQ_ATT, TK_ATT)
        h1, gates, idx, rank, counts = _out_proj(ym, ya, h0, w_out[l].astype(BF16), ln1_g[l], ln1_b[l],
                                                 w_router[l], b_router[l], tm_out, alpha)
        pos, tile_expert, n_used, tail_row, tile_slot, tile_next, n_rows = _moe_layout(
            idx[:, :TOP_K], rank[:, :TOP_K], counts[0, :N_EXPERTS].astype(I32), tr)
        xs = _dispatch(h1, pos, tail_row, n_used, n_rows, tm_disp, tr)
        ys = _experts(xs, tile_expert, n_used, tile_slot, tile_next, w_gate[l], b_gate[l], w_up[l], b_up[l],
                      w_down[l], b_down[l], tr)
        h = _combine(h1, gates, ys, pos, ln2_g[l], ln2_b[l], tm_comb, alpha)
    return h.reshape(B, T, D)
```
